```python
import jax, jax.numpy as jnp
from jax import lax
import numpy as np

D_MODEL = 2048
BATCH = 4
SEQ = 2048
DEPTH = 2
DEC_BATCH = 128
DEC_SEQ = 8
PAST_LEN = 16384
PAGE_SIZE = 128

D_MIX = D_MODEL
D_POOL = D_MIX // 2
D_CONV = D_MIX - D_POOL
POOL_WINDOWS = (2, 4, 8, 16)
N_POOL_GROUPS = len(POOL_WINDOWS)
POOL_GROUP = D_POOL // N_POOL_GROUPS
POOL_HIST = max(POOL_WINDOWS) - 1
CONV_WIDTH = 31
CONV_HIST = CONV_WIDTH - 1
CONV_HEADS = 8
CONV_HEAD_DIM = D_CONV // CONV_HEADS
N_MEM = 256
XA_HEADS = 4
XA_HEAD_DIM = D_MODEL // XA_HEADS
D_FF = ((8 * D_MODEL // 3 + 127) // 128) * 128
EPS = 1e-6

kernel_name = "hybrid_pool_conv_macaron_memxattn_step"


def rms_norm(x, g):
    xf = x.astype(jnp.float32)
    y = xf * lax.rsqrt(jnp.mean(xf * xf, axis=-1, keepdims=True) + EPS)
    return (y * g.astype(jnp.float32)).astype(x.dtype)


def swiglu_ffn(x, w_gate, w_up, w_down):
    return (jax.nn.silu(x @ w_gate) * (x @ w_up)) @ w_down


def pool_mix(p_ext, pos0, pool_w, pool_scale):
    B, L, _ = p_ext.shape
    T = L - POOL_HIST
    pf = p_ext.astype(jnp.float32)
    csum = jnp.concatenate([jnp.zeros((B, 1, D_POOL), jnp.float32), jnp.cumsum(pf, axis=1)], axis=1)
    end = csum[:, POOL_HIST + 1:]
    pos = pos0 + jnp.arange(T, dtype=jnp.int32)
    means = []
    for g, w in enumerate(POOL_WINDOWS):
        sl = slice(g * POOL_GROUP, (g + 1) * POOL_GROUP)
        start = csum[:, POOL_HIST + 1 - w: POOL_HIST + 1 - w + T, sl]
        cnt = jnp.minimum(pos + 1, w).astype(jnp.float32)[None, :, None]
        means.append((end[..., sl] - start) / cnt)
    mean = jnp.concatenate(means, axis=-1)
    d = (mean - pf[:, POOL_HIST:]).astype(p_ext.dtype).reshape(B, T, N_POOL_GROUPS, POOL_GROUP)
    y = jnp.einsum('btgc,gcd->btgd', d, pool_w).reshape(B, T, D_POOL)
    return y * pool_scale


def head_group_norm(y, g, b):
    B, T, C = y.shape
    yf = y.astype(jnp.float32).reshape(B, T, CONV_HEADS, CONV_HEAD_DIM)
    mu = jnp.mean(yf, axis=-1, keepdims=True)
    var = jnp.mean(jnp.square(yf - mu), axis=-1, keepdims=True)
    yn = ((yf - mu) * lax.rsqrt(var + EPS)).reshape(B, T, C)
    return (yn * g.astype(jnp.float32) + b.astype(jnp.float32)).astype(y.dtype)


def conv_mix(g_ext, conv_w, conv_b, conv_norm_g, conv_norm_b):
    y = lax.conv_general_dilated(
        g_ext, conv_w[:, None, :], window_strides=(1,), padding='VALID',
        dimension_numbers=('NWC', 'WIO', 'NWC'), feature_group_count=D_CONV) + conv_b
    return jax.nn.silu(head_group_norm(y, conv_norm_g, conv_norm_b))


def mem_kv(mem, mem_norm, w_mk, w_mv):
    B = mem.shape[0]
    m = rms_norm(mem, mem_norm)
    k = (m @ w_mk).reshape(B, N_MEM, XA_HEADS, XA_HEAD_DIM)
    v = (m @ w_mv).reshape(B, N_MEM, XA_HEADS, XA_HEAD_DIM)
    return k, v


def cross_attn(u, k, v, w_q, w_o):
    B, T, _ = u.shape
    q = (u @ w_q).reshape(B, T, XA_HEADS, XA_HEAD_DIM)
    s = jnp.einsum('bthd,bmhd->bhtm', q, k).astype(jnp.float32) * (XA_HEAD_DIM ** -0.5)
    p = jax.nn.softmax(s, axis=-1).astype(v.dtype)
    o = jnp.einsum('bhtm,bmhd->bthd', p, v).reshape(B, T, D_MODEL)
    return o @ w_o


def decoder_layer(x, pool_hist, conv_hist, mk, mv, pos0,
                  ffn1_norm, ffn1_w_gate, ffn1_w_up, ffn1_w_down,
                  mix_norm, w_in, pool_w, pool_scale, conv_w, conv_b, conv_norm_g, conv_norm_b, w_out,
                  xattn_norm, w_q, w_o,
                  ffn2_norm, ffn2_w_gate, ffn2_w_up, ffn2_w_down):
    h = x + 0.5 * swiglu_ffn(rms_norm(x, ffn1_norm), ffn1_w_gate, ffn1_w_up, ffn1_w_down)
    z = rms_norm(h, mix_norm) @ w_in
    p = z[..., :D_POOL]
    glu = z[..., D_POOL:D_POOL + D_CONV] * jax.nn.sigmoid(z[..., D_POOL + D_CONV:])
    p_ext = jnp.concatenate([pool_hist, p], axis=1)
    g_ext = jnp.concatenate([conv_hist, glu], axis=1)
    mixed = jnp.concatenate([pool_mix(p_ext, pos0, pool_w, pool_scale),
                             conv_mix(g_ext, conv_w, conv_b, conv_norm_g, conv_norm_b)], axis=-1)
    h = h + mixed @ w_out
    h = h + cross_attn(rms_norm(h, xattn_norm), mk, mv, w_q, w_o)
    h = h + 0.5 * swiglu_ffn(rms_norm(h, ffn2_norm), ffn2_w_gate, ffn2_w_up, ffn2_w_down)
    return h, p_ext[:, -POOL_HIST:], g_ext[:, -CONV_HIST:]


def setup_inputs(seed: int = 0) -> dict:
    key = jax.random.key(seed)
    ks = iter(jax.random.split(key, 64))
    f32 = jnp.float32

    def nrm(shape, scale=1.0):
        return jax.random.normal(next(ks), shape, f32) * scale

    def gain(shape):
        return 1.0 + 0.02 * jax.random.normal(next(ks), shape, f32)

    L, D = DEPTH, D_MODEL
    return {
        "x_prompt": nrm((BATCH, SEQ, D)),
        "x_sample": nrm((DEC_BATCH, DEC_SEQ, D)),
        "state_pool": nrm((L, DEC_BATCH, POOL_HIST, D_POOL)),
        "state_conv": nrm((L, DEC_BATCH, CONV_HIST, D_CONV), 0.5),
        "cache_mem_k": nrm((L, DEC_BATCH, N_MEM, XA_HEADS, XA_HEAD_DIM)),
        "cache_mem_v": nrm((L, DEC_BATCH, N_MEM, XA_HEADS, XA_HEAD_DIM)),
        "mem_prompt": nrm((BATCH, N_MEM, D)),
        "ffn1_norm": gain((L, D)),
        "ffn1_w_gate": nrm((L, D, D_FF), D ** -0.5),
        "ffn1_w_up": nrm((L, D, D_FF), D ** -0.5),
        "ffn1_w_down": nrm((L, D_FF, D), D_FF ** -0.5),
        "mix_norm": gain((L, D)),
        "w_in": nrm((L, D, D_POOL + 2 * D_CONV), D ** -0.5),
        "pool_w": nrm((L, N_POOL_GROUPS, POOL_GROUP, POOL_GROUP), POOL_GROUP ** -0.5),
        "pool_scale": 1.0 + 0.1 * nrm((L, D_POOL)),
        "conv_w": nrm((L, CONV_WIDTH, D_CONV), CONV_WIDTH ** -0.5),
        "conv_b": nrm((L, D_CONV), 0.02),
        "conv_norm_g": gain((L, D_CONV)),
        "conv_norm_b": nrm((L, D_CONV), 0.02),
        "w_out": nrm((L, D_POOL + D_CONV, D), (D_POOL + D_CONV) ** -0.5),
        "xattn_norm": gain((L, D)),
        "mem_norm": gain((L, D)),
        "w_q": nrm((L, D, XA_HEADS * XA_HEAD_DIM), D ** -0.5),
        "w_mk": nrm((L, D, XA_HEADS * XA_HEAD_DIM), D ** -0.5),
        "w_mv": nrm((L, D, XA_HEADS * XA_HEAD_DIM), D ** -0.5),
        "w_o": nrm((L, XA_HEADS * XA_HEAD_DIM, D), D ** -0.5),
        "ffn2_norm": gain((L, D)),
        "ffn2_w_gate": nrm((L, D, D_FF), D ** -0.5),
        "ffn2_w_up": nrm((L, D, D_FF), D ** -0.5),
        "ffn2_w_down": nrm((L, D_FF, D), D_FF ** -0.5),
        "final_norm": gain((D,)),
    }


def reference(x_prompt, x_sample, state_pool, state_conv, cache_mem_k, cache_mem_v, mem_prompt,
              ffn1_norm, ffn1_w_gate, ffn1_w_up, ffn1_w_down,
              mix_norm, w_in, pool_w, pool_scale, conv_w, conv_b, conv_norm_g, conv_norm_b, w_out,
              xattn_norm, mem_norm, w_q, w_mk, w_mv, w_o,
              ffn2_norm, ffn2_w_gate, ffn2_w_up, ffn2_w_down, final_norm):
    B = x_prompt.shape[0]
    hp, hs = x_prompt, x_sample
    pool_p, conv_p, mk_p, mv_p, pool_s, conv_s = [], [], [], [], [], []
    zero_pool = jnp.zeros((B, POOL_HIST, D_POOL), x_prompt.dtype)
    zero_conv = jnp.zeros((B, CONV_HIST, D_CONV), x_prompt.dtype)
    for l in range(DEPTH):
        lp = (ffn1_norm[l], ffn1_w_gate[l], ffn1_w_up[l], ffn1_w_down[l],
              mix_norm[l], w_in[l], pool_w[l], pool_scale[l], conv_w[l], conv_b[l],
              conv_norm_g[l], conv_norm_b[l], w_out[l],
              xattn_norm[l], w_q[l], w_o[l],
              ffn2_norm[l], ffn2_w_gate[l], ffn2_w_up[l], ffn2_w_down[l])
        mk, mv = mem_kv(mem_prompt, mem_norm[l], w_mk[l], w_mv[l])
        hp, ph, ch = decoder_layer(hp, zero_pool, zero_conv, mk, mv, 0, *lp)
        pool_p.append(ph); conv_p.append(ch); mk_p.append(mk); mv_p.append(mv)
        hs, ph, ch = decoder_layer(hs, state_pool[l], state_conv[l], cache_mem_k[l], cache_mem_v[l],
                                   PAST_LEN, *lp)
        pool_s.append(ph); conv_s.append(ch)
    y_prompt = rms_norm(hp, final_norm)
    y_sample = rms_norm(hs, final_norm)
    return (y_prompt, y_sample, jnp.stack(pool_p), jnp.stack(conv_p), jnp.stack(mk_p), jnp.stack(mv_p),
            jnp.stack(pool_s), jnp.stack(conv_s))
```

```python
import functools

import jax
import jax.numpy as jnp
from jax import lax
from jax.experimental import pallas as pl
from jax.experimental.pallas import tpu as pltpu

F32 = jnp.float32
BF16 = jnp.bfloat16

EPS = 1e-6
PAST_LEN = 16384
POOL_WINDOWS = (2, 4, 8, 16)
CONV_HEADS = 8

LANE = 128
SUBLANE = 8
VMEM_LIMIT = 56 * 1024 * 1024

FFN_TM = 1024
FFN_TF = 512
ROW_TM = 512
MIX_TS = 256
CONV_CHUNK = 32
SAMPLE_ATTN_BB = 4


def _params(*sem):
    return pltpu.CompilerParams(dimension_semantics=sem, vmem_limit_bytes=VMEM_LIMIT)


def _resident(block_shape, index_map):
    return pl.BlockSpec(block_shape, index_map, pipeline_mode=pl.Buffered(1))


def _rms(x, g):
    return x * lax.rsqrt(jnp.mean(x * x, axis=-1, keepdims=True) + EPS) * g


def _silu(x):
    return x * jax.nn.sigmoid(x)


def _tile(n, t):
    t = min(n, t)
    assert n % t == 0, (n, t)
    return t


def _ffn_kernel(x_ref, g_ref, wg_ref, wu_ref, wd_ref, fg_ref, o_ref, xn_ref, *, final_norm):
    j = pl.program_id(1)

    @pl.when(j == 0)
    def _():
        xn_ref[...] = _rms(x_ref[...], g_ref[...]).astype(BF16)
        o_ref[...] = jnp.zeros_like(o_ref)

    xn = xn_ref[...]
    gate = jnp.dot(xn, wg_ref[...], preferred_element_type=F32)
    up = jnp.dot(xn, wu_ref[...], preferred_element_type=F32)
    hmid = (_silu(gate) * up).astype(BF16)
    o_ref[...] += jnp.dot(hmid, wd_ref[...], preferred_element_type=F32)

    @pl.when(j == pl.num_programs(1) - 1)
    def _():
        y = x_ref[...] + 0.5 * o_ref[...]
        if final_norm:
            y = _rms(y, fg_ref[...])
        o_ref[...] = y


def _ffn(x, norm_g, wg, wu, wd, layer, final_g=None):
    n, d = x.shape
    fp = wg.shape[-1]
    tm = _tile(n, FFN_TM)
    tf = _tile(fp, FFN_TF)
    final_norm = final_g is not None
    if final_g is None:
        final_g = jnp.ones((1, d), F32)
    return pl.pallas_call(
        functools.partial(_ffn_kernel, final_norm=final_norm),
        out_shape=jax.ShapeDtypeStruct((n, d), F32),
        grid=(n // tm, fp // tf),
        in_specs=[
            _resident((tm, d), lambda i, j: (i, 0)),
            pl.BlockSpec((None, 1, d), lambda i, j: (layer, 0, 0)),
            pl.BlockSpec((None, d, tf), lambda i, j: (layer, 0, j)),
            pl.BlockSpec((None, d, tf), lambda i, j: (layer, 0, j)),
            pl.BlockSpec((None, tf, d), lambda i, j: (layer, j, 0)),
            pl.BlockSpec((1, d), lambda i, j: (0, 0)),
        ],
        out_specs=pl.BlockSpec((tm, d), lambda i, j: (i, 0)),
        scratch_shapes=[pltpu.VMEM((tm, d), BF16)],
        compiler_params=_params("parallel", "arbitrary"),
        name="ffn",
    )(x, norm_g, wg, wu, wd, final_g)


def _proj_in_kernel(x_ref, g_ref, w_ref, p_ref, glu_ref, *, d_pool, d_conv):
    xn = _rms(x_ref[...], g_ref[...]).astype(BF16)
    z = jnp.dot(xn, w_ref[...], preferred_element_type=F32)
    p_ref[...] = z[:, :d_pool]
    glu_ref[...] = z[:, d_pool:d_pool + d_conv] * jax.nn.sigmoid(z[:, d_pool + d_conv:])


def _proj_in(x, norm_g, w_in, layer, d_pool, d_conv):
    n, d = x.shape
    tm = _tile(n, ROW_TM)
    dz = w_in.shape[-1]
    return pl.pallas_call(
        functools.partial(_proj_in_kernel, d_pool=d_pool, d_conv=d_conv),
        out_shape=(jax.ShapeDtypeStruct((n, d_pool), F32), jax.ShapeDtypeStruct((n, d_conv), F32)),
        grid=(n // tm,),
        in_specs=[
            pl.BlockSpec((tm, d), lambda i: (i, 0)),
            pl.BlockSpec((None, 1, d), lambda i: (layer, 0, 0)),
            _resident((None, d, dz), lambda i: (layer, 0, 0)),
        ],
        out_specs=(pl.BlockSpec((tm, d_pool), lambda i: (i, 0)),
                   pl.BlockSpec((tm, d_conv), lambda i: (i, 0))),
        compiler_params=_params("parallel"),
        name="proj_in",
    )(x, norm_g, w_in)


def _rms_proj_kernel(x_ref, g_ref, w_ref, o_ref):
    xn = _rms(x_ref[...], g_ref[...]).astype(BF16)
    o_ref[...] = jnp.dot(xn, w_ref[...], preferred_element_type=F32)


def _rms_proj(x, norm_g, w, layer):
    n, d = x.shape
    dn = w.shape[-1]
    tm = _tile(n, ROW_TM)
    return pl.pallas_call(
        _rms_proj_kernel,
        out_shape=jax.ShapeDtypeStruct((n, dn), F32),
        grid=(n // tm,),
        in_specs=[
            pl.BlockSpec((tm, d), lambda i: (i, 0)),
            pl.BlockSpec((None, 1, d), lambda i: (layer, 0, 0)),
            _resident((None, d, dn), lambda i: (layer, 0, 0)),
        ],
        out_specs=pl.BlockSpec((tm, dn), lambda i: (i, 0)),
        compiler_params=_params("parallel"),
        name="rms_proj",
    )(x, norm_g, w)


def _matmul_res_kernel(res_ref, x_ref, w_ref, o_ref):
    o_ref[...] = res_ref[...] + jnp.dot(x_ref[...].astype(BF16), w_ref[...], preferred_element_type=F32)


def _matmul_res(res, x, w, layer):
    n, d = res.shape
    dk = x.shape[-1]
    tm = _tile(n, ROW_TM)
    return pl.pallas_call(
        _matmul_res_kernel,
        out_shape=jax.ShapeDtypeStruct((n, d), F32),
        grid=(n // tm,),
        in_specs=[
            pl.BlockSpec((tm, d), lambda i: (i, 0)),
            pl.BlockSpec((tm, dk), lambda i: (i, 0)),
            _resident((None, dk, d), lambda i: (layer, 0, 0)),
        ],
        out_specs=pl.BlockSpec((tm, d), lambda i: (i, 0)),
        compiler_params=_params("parallel"),
        name="matmul_res",
    )(res, x, w)


def _mix_prompt_kernel(p_ref, glu_ref, h_ref, pw_ref, ps_ref, cw_ref, cb_ref, ng_ref, nb_ref, wo_ref,
                       o_ref, pext, gsh, pool_out, conv_out, *, ts, pool_hist, conv_hist):
    t = pl.program_id(1)
    d_pool = p_ref.shape[-1]
    d_conv = glu_ref.shape[-1]
    pool_group = d_pool // len(POOL_WINDOWS)
    head_dim = d_conv // CONV_HEADS
    conv_width = conv_hist + 1
    ph = pext.shape[0] - ts
    gh = gsh.shape[1] - ts

    @pl.when(t == 0)
    def _():
        pext[0:ph, :] = jnp.zeros((ph, d_pool), F32)
        gsh[0, 0:gh, :] = jnp.zeros((gh, d_conv), F32)

    pext[ph:ph + ts, :] = p_ref[...]
    gsh[0, gh:gh + ts, :] = glu_ref[...]
    for r in range(1, SUBLANE):
        gsh[r, 0:gh + ts - SUBLANE, :] = gsh[0, r:r + gh + ts - SUBLANE, :]

    pos = t * ts + lax.broadcasted_iota(jnp.int32, (ts, 1), 0)
    for g, w in enumerate(POOL_WINDOWS):
        sl = slice(g * pool_group, (g + 1) * pool_group)
        cur = pext[ph:ph + ts, sl]
        acc = cur
        for k in range(1, w):
            acc = acc + pext[ph - k:ph - k + ts, sl]
        cnt = jnp.minimum(pos + 1, w).astype(F32)
        dlt = (acc / cnt - cur).astype(BF16)
        y = jnp.dot(dlt, pw_ref[g], preferred_element_type=F32) * ps_ref[:, sl]
        pool_out[:, sl] = y.astype(BF16)

    def conv_chunk(c, carry):
        r0 = pl.multiple_of(c * CONV_CHUNK, CONV_CHUNK)
        acc = jnp.broadcast_to(cb_ref[...], (CONV_CHUNK, d_conv))
        for k in range(conv_width):
            off = gh - conv_hist + k
            rows = pl.ds(r0 + (off // SUBLANE) * SUBLANE, CONV_CHUNK)
            acc = acc + gsh[off % SUBLANE, rows, :] * cw_ref[k:k + 1, :]
        for hh in range(CONV_HEADS):
            hs = slice(hh * head_dim, (hh + 1) * head_dim)
            yh = acc[:, hs]
            mu = jnp.mean(yh, axis=-1, keepdims=True)
            yc = yh - mu
            var = jnp.mean(yc * yc, axis=-1, keepdims=True)
            yn = yc * lax.rsqrt(var + EPS) * ng_ref[:, hs] + nb_ref[:, hs]
            conv_out[pl.ds(r0, CONV_CHUNK), hs] = _silu(yn).astype(BF16)
        return carry

    lax.fori_loop(0, ts // CONV_CHUNK, conv_chunk, 0)

    mixed = (jnp.dot(pool_out[...], wo_ref[0:d_pool, :], preferred_element_type=F32)
             + jnp.dot(conv_out[...], wo_ref[d_pool:d_pool + d_conv, :], preferred_element_type=F32))
    o_ref[...] = h_ref[...] + mixed

    pext[0:ph, :] = pext[ts:ts + ph, :]
    gsh[0, 0:gh, :] = gsh[0, ts:ts + gh, :]


def _mix_prompt(p, glu, h, batch, seq, pool_w, pool_scale, conv_w, conv_b, norm_g, norm_b, w_out, layer,
                pool_hist, conv_hist):
    n, d = h.shape
    d_pool, d_conv = p.shape[-1], glu.shape[-1]
    ts = _tile(seq, MIX_TS)
    assert ts % CONV_CHUNK == 0
    nt = seq // ts
    ph = -(-pool_hist // SUBLANE) * SUBLANE
    gh = -(-conv_hist // SUBLANE) * SUBLANE
    row = lambda b, t: (b * nt + t, 0)
    vec = lambda b, t: (layer, 0, 0)
    return pl.pallas_call(
        functools.partial(_mix_prompt_kernel, ts=ts, pool_hist=pool_hist, conv_hist=conv_hist),
        out_shape=jax.ShapeDtypeStruct((n, d), F32),
        grid=(batch, nt),
        in_specs=[
            pl.BlockSpec((ts, d_pool), row),
            pl.BlockSpec((ts, d_conv), row),
            pl.BlockSpec((ts, d), row),
            _resident((None,) + pool_w.shape[1:], lambda b, t: (layer, 0, 0, 0)),
            pl.BlockSpec((None, 1, d_pool), vec),
            pl.BlockSpec((None, conv_hist + 1, d_conv), vec),
            pl.BlockSpec((None, 1, d_conv), vec),
            pl.BlockSpec((None, 1, d_conv), vec),
            pl.BlockSpec((None, 1, d_conv), vec),
            _resident((None, d_pool + d_conv, d), vec),
        ],
        out_specs=pl.BlockSpec((ts, d), row),
        scratch_shapes=[
            pltpu.VMEM((ph + ts, d_pool), F32),
            pltpu.VMEM((SUBLANE, gh + ts, d_conv), F32),
            pltpu.VMEM((ts, d_pool), BF16),
            pltpu.VMEM((ts, d_conv), BF16),
        ],
        compiler_params=_params("arbitrary", "arbitrary"),
        name="mix_prompt",
    )(p, glu, h, pool_w, pool_scale, conv_w, conv_b, norm_g, norm_b, w_out)


def _mix_sample_kernel(pe_ref, ge_ref, pw_ref, ps_ref, cw_ref, cb_ref, ng_ref, nb_ref,
                       po_ref, co_ref, dbuf, *, steps, pool_hist, conv_hist):
    g = pl.program_id(0)
    bs, cb = pe_ref.shape[1], pe_ref.shape[2]
    head_dim = ng_ref.shape[-1] * len(POOL_WINDOWS) // CONV_HEADS
    conv_width = conv_hist + 1

    for t in range(steps):
        cur = pe_ref[pool_hist + t]
        acc = cur
        dlt = jnp.zeros_like(cur)
        k = 1
        for gi, w in enumerate(POOL_WINDOWS):
            while k < w:
                acc = acc + pe_ref[pool_hist + t - k]
                k += 1
            cnt = float(min(PAST_LEN + t + 1, w))
            dlt = jnp.where(g == gi, acc / cnt - cur, dlt)
        dbuf[t] = dlt.astype(BF16)

        y = jnp.broadcast_to(cb_ref[...], (bs, cb))
        for k in range(conv_width):
            y = y + ge_ref[t + k] * cw_ref[k:k + 1, :]
        for hh in range(cb // head_dim):
            hs = slice(hh * head_dim, (hh + 1) * head_dim)
            yh = y[:, hs]
            mu = jnp.mean(yh, axis=-1, keepdims=True)
            yc = yh - mu
            var = jnp.mean(yc * yc, axis=-1, keepdims=True)
            yn = yc * lax.rsqrt(var + EPS) * ng_ref[:, hs] + nb_ref[:, hs]
            co_ref[t, :, hs] = _silu(yn).astype(BF16)

    dall = dbuf[...].reshape(steps * bs, cb)
    y = jnp.dot(dall, pw_ref[...], preferred_element_type=F32) * ps_ref[...]
    po_ref[...] = y.reshape(steps, bs, cb).astype(BF16)


def _mix_sample(pext_t, gext_t, pool_w, pool_scale, conv_w, conv_b, norm_g, norm_b, layer, steps,
                pool_hist, conv_hist):
    rp, bs, d_pool = pext_t.shape
    rg, _, d_conv = gext_t.shape
    ng = len(POOL_WINDOWS)
    cb = d_pool // ng
    assert d_conv // ng == cb and cb % (d_conv // CONV_HEADS) == 0
    vec = lambda g: (layer, 0, g)
    return pl.pallas_call(
        functools.partial(_mix_sample_kernel, steps=steps, pool_hist=pool_hist, conv_hist=conv_hist),
        out_shape=(jax.ShapeDtypeStruct((steps, bs, d_pool), BF16),
                   jax.ShapeDtypeStruct((steps, bs, d_conv), BF16)),
        grid=(ng,),
        in_specs=[
            pl.BlockSpec((rp, bs, cb), lambda g: (0, 0, g)),
            pl.BlockSpec((rg, bs, cb), lambda g: (0, 0, g)),
            pl.BlockSpec((None, None, cb, cb), lambda g: (layer, g, 0, 0)),
            pl.BlockSpec((None, 1, cb), vec),
            pl.BlockSpec((None, conv_hist + 1, cb), vec),
            pl.BlockSpec((None, 1, cb), vec),
            pl.BlockSpec((None, 1, cb), vec),
            pl.BlockSpec((None, 1, cb), vec),
        ],
        out_specs=(pl.BlockSpec((steps, bs, cb), lambda g: (0, 0, g)),
                   pl.BlockSpec((steps, bs, cb), lambda g: (0, 0, g))),
        scratch_shapes=[pltpu.VMEM((steps, bs, cb), BF16)],
        compiler_params=_params("parallel"),
        name="mix_sample",
    )(pext_t, gext_t, pool_w, pool_scale, conv_w, conv_b, norm_g, norm_b)


def _softmax(s):
    e = jnp.exp(s - jnp.max(s, axis=-1, keepdims=True))
    return e / jnp.sum(e, axis=-1, keepdims=True)


def _attn_prompt_kernel(h_ref, g_ref, wq_ref, k_ref, v_ref, wo_ref, o_ref, *, heads):
    x = h_ref[...]
    d = x.shape[-1]
    hd = d // heads
    q = jnp.dot(_rms(x, g_ref[...]).astype(BF16), wq_ref[...], preferred_element_type=F32)
    acc = x
    for hh in range(heads):
        hs = slice(hh * hd, (hh + 1) * hd)
        s = lax.dot_general(q[:, hs].astype(BF16), k_ref[:, hs].astype(BF16),
                            (((1,), (1,)), ((), ())), preferred_element_type=F32) * (hd ** -0.5)
        p = _softmax(s).astype(BF16)
        oh = jnp.dot(p, v_ref[:, hs].astype(BF16), preferred_element_type=F32)
        acc = acc + jnp.dot(oh.astype(BF16), wo_ref[hs, :], preferred_element_type=F32)
    o_ref[...] = acc


def _attn_prompt(h, batch, seq, norm_g, w_q, mk, mv, w_o, layer, heads):
    n, d = h.shape
    n_mem = mk.shape[1]
    tq = _tile(seq, ROW_TM)
    nt = seq // tq
    row = lambda b, t: (b * nt + t, 0)
    return pl.pallas_call(
        functools.partial(_attn_prompt_kernel, heads=heads),
        out_shape=jax.ShapeDtypeStruct((n, d), F32),
        grid=(batch, nt),
        in_specs=[
            pl.BlockSpec((tq, d), row),
            pl.BlockSpec((None, 1, d), lambda b, t: (layer, 0, 0)),
            _resident((None, d, d), lambda b, t: (layer, 0, 0)),
            pl.BlockSpec((None, n_mem, d), lambda b, t: (b, 0, 0)),
            pl.BlockSpec((None, n_mem, d), lambda b, t: (b, 0, 0)),
            _resident((None, d, d), lambda b, t: (layer, 0, 0)),
        ],
        out_specs=pl.BlockSpec((tq, d), row),
        compiler_params=_params("parallel", "parallel"),
        name="attn_prompt",
    )(h, norm_g, w_q, mk, mv, w_o)


def _attn_sample_kernel(q_ref, k_ref, v_ref, o_ref, *, heads):
    d = q_ref.shape[-1]
    hd = d // heads
    for hh in range(heads):
        hs = slice(hh * hd, (hh + 1) * hd)
        s = jnp.einsum("bqd,bkd->bqk", q_ref[:, :, hs].astype(BF16), k_ref[:, :, hs].astype(BF16),
                       preferred_element_type=F32) * (hd ** -0.5)
        p = _softmax(s).astype(BF16)
        o_ref[:, :, hs] = jnp.einsum("bqk,bkd->bqd", p, v_ref[:, :, hs].astype(BF16),
                                     preferred_element_type=F32)


def _attn_sample(q, cache_k, cache_v, layer, heads):
    bs, t, d = q.shape
    n_mem = cache_k.shape[2]
    bb = _tile(bs, SAMPLE_ATTN_BB)
    kv = pl.BlockSpec((None, bb, n_mem, d), lambda i: (layer, i, 0, 0))
    return pl.pallas_call(
        functools.partial(_attn_sample_kernel, heads=heads),
        out_shape=jax.ShapeDtypeStruct((bs, t, d), F32),
        grid=(bs // bb,),
        in_specs=[pl.BlockSpec((bb, t, d), lambda i: (i, 0, 0)), kv, kv],
        out_specs=pl.BlockSpec((bb, t, d), lambda i: (i, 0, 0)),
        compiler_params=_params("parallel"),
        name="attn_sample",
    )(q, cache_k, cache_v)


def _pad_to(w, axis, mult):
    pad = -w.shape[axis] % mult
    if pad == 0:
        return w
    widths = [(0, 0)] * w.ndim
    widths[axis] = (0, pad)
    return jnp.pad(w, widths)


def kernel(x_prompt, x_sample, state_pool, state_conv, cache_mem_k, cache_mem_v, mem_prompt, ffn1_norm, ffn1_w_gate, ffn1_w_up, ffn1_w_down, mix_norm, w_in, pool_w, pool_scale, conv_w, conv_b, conv_norm_g, conv_norm_b, w_out, xattn_norm, mem_norm, w_q, w_mk, w_mv, w_o, ffn2_norm, ffn2_w_gate, ffn2_w_up, ffn2_w_down, final_norm):
    batch, seq, d = x_prompt.shape
    bs, steps, _ = x_sample.shape
    depth = state_pool.shape[0]
    pool_hist, d_pool = state_pool.shape[2:]
    conv_hist, d_conv = state_conv.shape[2:]
    n_mem, heads, head_dim = cache_mem_k.shape[2:]

    bf = lambda w: w.astype(BF16)
    ffn_w = []
    for wg, wu, wd in ((ffn1_w_gate, ffn1_w_up, ffn1_w_down), (ffn2_w_gate, ffn2_w_up, ffn2_w_down)):
        ffn_w.append((_pad_to(bf(wg), 2, FFN_TF), _pad_to(bf(wu), 2, FFN_TF), _pad_to(bf(wd), 1, FFN_TF)))
    w_in_b, pool_w_b, w_out_b = bf(w_in), bf(pool_w), bf(w_out)
    w_q_b, w_mk_b, w_mv_b, w_o_b = bf(w_q), bf(w_mk), bf(w_mv), bf(w_o)

    vec = lambda v: v.reshape(v.shape[0], 1, v.shape[1])
    ffn_norms = (vec(ffn1_norm), vec(ffn2_norm))
    mix_norm_v, xattn_norm_v, mem_norm_v = vec(mix_norm), vec(xattn_norm), vec(mem_norm)
    pool_scale_v, conv_b_v, norm_g_v, norm_b_v = vec(pool_scale), vec(conv_b), vec(conv_norm_g), vec(conv_norm_b)
    final_g = final_norm.reshape(1, d)

    cache_k = cache_mem_k.reshape(depth, bs, n_mem, d)
    cache_v = cache_mem_v.reshape(depth, bs, n_mem, d)
    mem = mem_prompt.reshape(batch * n_mem, d)

    hp = x_prompt.reshape(batch * seq, d)
    hs = x_sample.reshape(bs * steps, d)
    pool_p, conv_p, mk_p, mv_p, pool_s, conv_s = [], [], [], [], [], []

    for l in range(depth):
        last = l == depth - 1
        hp = _ffn(hp, ffn_norms[0], *ffn_w[0], l)
        hs = _ffn(hs, ffn_norms[0], *ffn_w[0], l)

        p, glu = _proj_in(hp, mix_norm_v, w_in_b, l, d_pool, d_conv)
        pool_p.append(p.reshape(batch, seq, d_pool)[:, seq - pool_hist:])
        conv_p.append(glu.reshape(batch, seq, d_conv)[:, seq - conv_hist:])
        hp = _mix_prompt(p, glu, hp, batch, seq, pool_w_b, pool_scale_v, conv_w, conv_b_v, norm_g_v, norm_b_v,
                         w_out_b, l, pool_hist, conv_hist)

        p, glu = _proj_in(hs, mix_norm_v, w_in_b, l, d_pool, d_conv)
        p_ext = jnp.concatenate([state_pool[l], p.reshape(bs, steps, d_pool)], axis=1)
        g_ext = jnp.concatenate([state_conv[l], glu.reshape(bs, steps, d_conv)], axis=1)
        pool_s.append(p_ext[:, -pool_hist:])
        conv_s.append(g_ext[:, -conv_hist:])
        po, co = _mix_sample(p_ext.transpose(1, 0, 2), g_ext.transpose(1, 0, 2), pool_w_b, pool_scale_v, conv_w,
                             conv_b_v, norm_g_v, norm_b_v, l, steps, pool_hist, conv_hist)
        mixed = jnp.concatenate([po, co], axis=-1).transpose(1, 0, 2).reshape(bs * steps, d_pool + d_conv)
        hs = _matmul_res(hs, mixed, w_out_b, l)

        mk = _rms_proj(mem, mem_norm_v, w_mk_b, l)
        mv = _rms_proj(mem, mem_norm_v, w_mv_b, l)
        mk_p.append(mk.reshape(batch, n_mem, heads, head_dim))
        mv_p.append(mv.reshape(batch, n_mem, heads, head_dim))
        hp = _attn_prompt(hp, batch, seq, xattn_norm_v, w_q_b, mk.reshape(batch, n_mem, d),
                          mv.reshape(batch, n_mem, d), w_o_b, l, heads)
        q = _rms_proj(hs, xattn_norm_v, w_q_b, l)
        o = _attn_sample(q.reshape(bs, steps, d), cache_k, cache_v, l, heads)
        hs = _matmul_res(hs, o.reshape(bs * steps, d), w_o_b, l)

        hp = _ffn(hp, ffn_norms[1], *ffn_w[1], l, final_g if last else None)
        hs = _ffn(hs, ffn_norms[1], *ffn_w[1], l, final_g if last else None)

    return (hp.reshape(batch, seq, d), hs.reshape(bs, steps, d), jnp.stack(pool_p), jnp.stack(conv_p),
            jnp.stack(mk_p), jnp.stack(mv_p), jnp.stack(pool_s), jnp.stack(conv_s))
```

```python
import functools

import jax
import jax.numpy as jnp
from jax import lax
from jax.experimental import pallas as pl
from jax.experimental.pallas import tpu as pltpu

F32 = jnp.float32
BF16 = jnp.bfloat16

EPS = 1e-6
PAST_LEN = 16384
POOL_WINDOWS = (2, 4, 8, 16)
CONV_HEADS = 8

LANE = 128
SUBLANE = 8
VMEM_LIMIT = 56 * 1024 * 1024

FFN_TM = 1024
FFN_TF = 512
ROW_TM = 512
MIX_TS = 512
CONV_CHUNK = 64
SAMPLE_ATTN_BB = 4


def _params(*sem):
    return pltpu.CompilerParams(dimension_semantics=sem, vmem_limit_bytes=VMEM_LIMIT)


def _resident(block_shape, index_map):
    return pl.BlockSpec(block_shape, index_map, pipeline_mode=pl.Buffered(1))


def _rms(x, g):
    return x * lax.rsqrt(jnp.mean(x * x, axis=-1, keepdims=True) + EPS) * g


def _silu(x):
    return x * jax.nn.sigmoid(x)


def _tile(n, t):
    t = min(n, t)
    assert n % t == 0, (n, t)
    return t


def _ffn_kernel(x_ref, g_ref, wg_ref, wu_ref, wd_ref, fg_ref, o_ref, xn_ref, *, final_norm, d_ff):
    j = pl.program_id(1)
    nj = pl.num_programs(1)
    tf = wg_ref.shape[-1]

    @pl.when(j == 0)
    def _():
        xn_ref[...] = _rms(x_ref[...], g_ref[...]).astype(BF16)
        o_ref[...] = jnp.zeros_like(o_ref)

    xn = xn_ref[...]
    gate = jnp.dot(xn, wg_ref[0], preferred_element_type=F32)
    up = jnp.dot(xn, wu_ref[0], preferred_element_type=F32)
    done = jnp.where(j == nj - 1, nj * tf - d_ff, 0)
    col = lax.broadcasted_iota(jnp.int32, (1, tf), 1)
    hmid = jnp.where(col >= done, _silu(gate) * up, 0.0).astype(BF16)
    o_ref[...] += jnp.dot(hmid, wd_ref[0], preferred_element_type=F32)

    @pl.when(j == nj - 1)
    def _():
        y = x_ref[...] + 0.5 * o_ref[...]
        if final_norm:
            y = _rms(y, fg_ref[...])
        o_ref[...] = y


def _ffn(x, norm_g, wg, wu, wd, layer, final_g=None):
    n, d = x.shape
    d_ff = wg.shape[-1]
    tm = _tile(n, FFN_TM)
    tf = min(FFN_TF, d_ff)
    assert tf % LANE == 0 and d_ff % LANE == 0
    final_norm = final_g is not None
    if final_g is None:
        final_g = jnp.ones((1, d), F32)
    start = lambda j: pl.multiple_of(jnp.minimum(j * tf, d_ff - tf), LANE)
    el = pl.Element
    return pl.pallas_call(
        functools.partial(_ffn_kernel, final_norm=final_norm, d_ff=d_ff),
        out_shape=jax.ShapeDtypeStruct((n, d), F32),
        grid=(n // tm, pl.cdiv(d_ff, tf)),
        in_specs=[
            _resident((tm, d), lambda i, j: (i, 0)),
            pl.BlockSpec((None, 1, d), lambda i, j: (layer, 0, 0)),
            pl.BlockSpec((el(1), el(d), el(tf)), lambda i, j: (layer, 0, start(j))),
            pl.BlockSpec((el(1), el(d), el(tf)), lambda i, j: (layer, 0, start(j))),
            pl.BlockSpec((el(1), el(tf), el(d)), lambda i, j: (layer, start(j), 0)),
            pl.BlockSpec((1, d), lambda i, j: (0, 0)),
        ],
        out_specs=pl.BlockSpec((tm, d), lambda i, j: (i, 0)),
        scratch_shapes=[pltpu.VMEM((tm, d), BF16)],
        compiler_params=_params("parallel", "arbitrary"),
        name="ffn",
    )(x, norm_g, wg, wu, wd, final_g)


def _proj_in_kernel(x_ref, g_ref, w_ref, p_ref, glu_ref, *, d_pool, d_conv):
    xn = _rms(x_ref[...], g_ref[...]).astype(BF16)
    z = jnp.dot(xn, w_ref[...], preferred_element_type=F32)
    p_ref[...] = z[:, :d_pool]
    glu_ref[...] = z[:, d_pool:d_pool + d_conv] * jax.nn.sigmoid(z[:, d_pool + d_conv:])


def _proj_in(x, norm_g, w_in, layer, d_pool, d_conv):
    n, d = x.shape
    tm = _tile(n, ROW_TM)
    dz = w_in.shape[-1]
    return pl.pallas_call(
        functools.partial(_proj_in_kernel, d_pool=d_pool, d_conv=d_conv),
        out_shape=(jax.ShapeDtypeStruct((n, d_pool), F32), jax.ShapeDtypeStruct((n, d_conv), F32)),
        grid=(n // tm,),
        in_specs=[
            pl.BlockSpec((tm, d), lambda i: (i, 0)),
            pl.BlockSpec((None, 1, d), lambda i: (layer, 0, 0)),
            _resident((None, d, dz), lambda i: (layer, 0, 0)),
        ],
        out_specs=(pl.BlockSpec((tm, d_pool), lambda i: (i, 0)),
                   pl.BlockSpec((tm, d_conv), lambda i: (i, 0))),
        compiler_params=_params("parallel"),
        name="proj_in",
    )(x, norm_g, w_in)


def _rms_proj_kernel(x_ref, g_ref, w_ref, o_ref):
    xn = _rms(x_ref[...], g_ref[...]).astype(BF16)
    o_ref[...] = jnp.dot(xn, w_ref[...], preferred_element_type=F32)


def _rms_proj(x, norm_g, w, layer):
    n, d = x.shape
    dn = w.shape[-1]
    tm = _tile(n, ROW_TM)
    return pl.pallas_call(
        _rms_proj_kernel,
        out_shape=jax.ShapeDtypeStruct((n, dn), F32),
        grid=(n // tm,),
        in_specs=[
            pl.BlockSpec((tm, d), lambda i: (i, 0)),
            pl.BlockSpec((None, 1, d), lambda i: (layer, 0, 0)),
            _resident((None, d, dn), lambda i: (layer, 0, 0)),
        ],
        out_specs=pl.BlockSpec((tm, dn), lambda i: (i, 0)),
        compiler_params=_params("parallel"),
        name="rms_proj",
    )(x, norm_g, w)


def _matmul_res_kernel(res_ref, x_ref, w_ref, o_ref):
    o_ref[...] = res_ref[...] + jnp.dot(x_ref[...].astype(BF16), w_ref[...], preferred_element_type=F32)


def _matmul_res(res, x, w, layer):
    n, d = res.shape
    dk = x.shape[-1]
    tm = _tile(n, ROW_TM)
    return pl.pallas_call(
        _matmul_res_kernel,
        out_shape=jax.ShapeDtypeStruct((n, d), F32),
        grid=(n // tm,),
        in_specs=[
            pl.BlockSpec((tm, d), lambda i: (i, 0)),
            pl.BlockSpec((tm, dk), lambda i: (i, 0)),
            _resident((None, dk, d), lambda i: (layer, 0, 0)),
        ],
        out_specs=pl.BlockSpec((tm, d), lambda i: (i, 0)),
        compiler_params=_params("parallel"),
        name="matmul_res",
    )(res, x, w)


def _head_norm_swish(y, g, b):
    yc = y - jnp.mean(y, axis=-1, keepdims=True)
    var = jnp.mean(yc * yc, axis=-1, keepdims=True)
    return _silu(yc * lax.rsqrt(var + EPS) * g + b)


def _mix_prompt_kernel(p_ref, glu_ref, h_ref, pw_ref, ps_ref, cw_ref, cb_ref, ng_ref, nb_ref, wo_ref,
                       o_ref, pext, gs, ys, pool_out, *, ts, conv_hist):
    t = pl.program_id(1)
    d_pool = p_ref.shape[-1]
    pool_group = d_pool // len(POOL_WINDOWS)
    heads, _, hd = gs.shape
    conv_width = conv_hist + 1
    ph = pext.shape[0] - ts
    gh = gs.shape[1] - ts

    @pl.when(t == 0)
    def _():
        pext[0:ph, :] = jnp.zeros((ph, d_pool), F32)
        gs[:, 0:gh, :] = jnp.zeros((heads, gh, hd), F32)

    pext[ph:ph + ts, :] = p_ref[...]
    for c in range(heads):
        gs[c, gh:gh + ts, :] = glu_ref[:, c * hd:(c + 1) * hd]

    pos = t * ts + lax.broadcasted_iota(jnp.int32, (ts, 1), 0)
    for g, w in enumerate(POOL_WINDOWS):
        sl = slice(g * pool_group, (g + 1) * pool_group)
        cur = pext[ph:ph + ts, sl]
        acc = cur
        for k in range(1, w):
            acc = acc + pext[ph - k:ph - k + ts, sl]
        cnt = jnp.minimum(pos + 1, w).astype(F32)
        dlt = (acc / cnt - cur).astype(BF16)
        y = jnp.dot(dlt, pw_ref[g], preferred_element_type=F32) * ps_ref[:, sl]
        pool_out[:, sl] = y.astype(BF16)

    groups = CONV_CHUNK // (2 * SUBLANE)

    def conv_chunk(i, carry):
        r0 = pl.multiple_of(i * CONV_CHUNK, CONV_CHUNK)
        for c in range(heads):
            hs = slice(c * hd, (c + 1) * hd)
            bias = jnp.broadcast_to(cb_ref[:, hs], (SUBLANE, hd))
            even = [bias] * groups
            odd = [bias] * groups
            for j in range(conv_width + 1):
                rows = [gs[c, pl.ds(r0 + q * 2 * SUBLANE + (gh - conv_hist) + j, SUBLANE, stride=2), :]
                        for q in range(groups)]
                if j < conv_width:
                    wk = cw_ref[j, c]
                    even = [a + x * wk for a, x in zip(even, rows)]
                if j >= 1:
                    wk = cw_ref[j - 1, c]
                    odd = [a + x * wk for a, x in zip(odd, rows)]
            for q in range(groups):
                for phase, acc in enumerate((even[q], odd[q])):
                    y = _head_norm_swish(acc, ng_ref[:, hs], nb_ref[:, hs])
                    ys[c, pl.ds(r0 + q * 2 * SUBLANE + phase, SUBLANE, stride=2), :] = y
        return carry

    lax.fori_loop(0, ts // CONV_CHUNK, conv_chunk, 0)

    conv_out = jnp.concatenate([ys[c] for c in range(heads)], axis=1).astype(BF16)
    mixed = (jnp.dot(pool_out[...], wo_ref[0:d_pool, :], preferred_element_type=F32)
             + jnp.dot(conv_out, wo_ref[d_pool:, :], preferred_element_type=F32))
    o_ref[...] = h_ref[...] + mixed

    pext[0:ph, :] = pext[ts:ts + ph, :]
    for c in range(heads):
        gs[c, 0:gh, :] = gs[c, ts:ts + gh, :]


def _mix_prompt(p, glu, h, batch, seq, pool_w, pool_scale, conv_wb, conv_b, norm_g, norm_b, w_out, layer,
                pool_hist, conv_hist):
    n, d = h.shape
    d_pool, d_conv = p.shape[-1], glu.shape[-1]
    heads, hd = conv_wb.shape[2], conv_wb.shape[4]
    assert heads * hd == d_conv and hd == LANE
    ts = _tile(seq, MIX_TS)
    assert ts % CONV_CHUNK == 0
    nt = seq // ts
    ph = -(-pool_hist // SUBLANE) * SUBLANE
    gh = -(-conv_hist // SUBLANE) * SUBLANE
    row = lambda b, t: (b * nt + t, 0)
    vec = lambda b, t: (layer, 0, 0)
    return pl.pallas_call(
        functools.partial(_mix_prompt_kernel, ts=ts, conv_hist=conv_hist),
        out_shape=jax.ShapeDtypeStruct((n, d), F32),
        grid=(batch, nt),
        in_specs=[
            pl.BlockSpec((ts, d_pool), row),
            pl.BlockSpec((ts, d_conv), row),
            pl.BlockSpec((ts, d), row),
            _resident((None,) + pool_w.shape[1:], lambda b, t: (layer, 0, 0, 0)),
            pl.BlockSpec((None, 1, d_pool), vec),
            _resident((None,) + conv_wb.shape[1:], lambda b, t: (layer, 0, 0, 0, 0)),
            pl.BlockSpec((None, 1, d_conv), vec),
            pl.BlockSpec((None, 1, d_conv), vec),
            pl.BlockSpec((None, 1, d_conv), vec),
            _resident((None, d_pool + d_conv, d), vec),
        ],
        out_specs=pl.BlockSpec((ts, d), row),
        scratch_shapes=[
            pltpu.VMEM((ph + ts, d_pool), F32),
            pltpu.VMEM((heads, gh + ts, hd), F32),
            pltpu.VMEM((heads, ts, hd), F32),
            pltpu.VMEM((ts, d_pool), BF16),
        ],
        compiler_params=_params("arbitrary", "arbitrary"),
        name="mix_prompt",
    )(p, glu, h, pool_w, pool_scale, conv_wb, conv_b, norm_g, norm_b, w_out)


def _mix_sample_kernel(pe_ref, ge_ref, pw_ref, ps_ref, cw_ref, cb_ref, ng_ref, nb_ref,
                       po_ref, co_ref, dbuf, *, steps, pool_hist, conv_hist):
    g = pl.program_id(0)
    bs, cb = pe_ref.shape[1], pe_ref.shape[2]
    head_dim = ng_ref.shape[-1] * len(POOL_WINDOWS) // CONV_HEADS
    conv_width = conv_hist + 1

    for t in range(steps):
        cur = pe_ref[pool_hist + t]
        acc = cur
        dlt = jnp.zeros_like(cur)
        k = 1
        for gi, w in enumerate(POOL_WINDOWS):
            while k < w:
                acc = acc + pe_ref[pool_hist + t - k]
                k += 1
            cnt = float(min(PAST_LEN + t + 1, w))
            dlt = jnp.where(g == gi, acc / cnt - cur, dlt)
        dbuf[t] = dlt.astype(BF16)

        y = jnp.broadcast_to(cb_ref[...], (bs, cb))
        for k in range(conv_width):
            y = y + ge_ref[t + k] * cw_ref[k:k + 1, :]
        for hh in range(cb // head_dim):
            hs = slice(hh * head_dim, (hh + 1) * head_dim)
            co_ref[t, :, hs] = _head_norm_swish(y[:, hs], ng_ref[:, hs], nb_ref[:, hs]).astype(BF16)

    dall = dbuf[...].reshape(steps * bs, cb)
    y = jnp.dot(dall, pw_ref[...], preferred_element_type=F32) * ps_ref[...]
    po_ref[...] = y.reshape(steps, bs, cb).astype(BF16)


def _mix_sample(pext_t, gext_t, pool_w, pool_scale, conv_w, conv_b, norm_g, norm_b, layer, steps,
                pool_hist, conv_hist):
    rp, bs, d_pool = pext_t.shape
    rg, _, d_conv = gext_t.shape
    ng = len(POOL_WINDOWS)
    cb = d_pool // ng
    assert d_conv // ng == cb and cb % (d_conv // CONV_HEADS) == 0
    vec = lambda g: (layer, 0, g)
    return pl.pallas_call(
        functools.partial(_mix_sample_kernel, steps=steps, pool_hist=pool_hist, conv_hist=conv_hist),
        out_shape=(jax.ShapeDtypeStruct((steps, bs, d_pool), BF16),
                   jax.ShapeDtypeStruct((steps, bs, d_conv), BF16)),
        grid=(ng,),
        in_specs=[
            pl.BlockSpec((rp, bs, cb), lambda g: (0, 0, g)),
            pl.BlockSpec((rg, bs, cb), lambda g: (0, 0, g)),
            pl.BlockSpec((None, None, cb, cb), lambda g: (layer, g, 0, 0)),
            pl.BlockSpec((None, 1, cb), vec),
            pl.BlockSpec((None, conv_hist + 1, cb), vec),
            pl.BlockSpec((None, 1, cb), vec),
            pl.BlockSpec((None, 1, cb), vec),
            pl.BlockSpec((None, 1, cb), vec),
        ],
        out_specs=(pl.BlockSpec((steps, bs, cb), lambda g: (0, 0, g)),
                   pl.BlockSpec((steps, bs, cb), lambda g: (0, 0, g))),
        scratch_shapes=[pltpu.VMEM((steps, bs, cb), BF16)],
        compiler_params=_params("parallel"),
        name="mix_sample",
    )(pext_t, gext_t, pool_w, pool_scale, conv_w, conv_b, norm_g, norm_b)


def _softmax(s):
    e = jnp.exp(s - jnp.max(s, axis=-1, keepdims=True))
    return e / jnp.sum(e, axis=-1, keepdims=True)


def _attn_prompt_kernel(h_ref, g_ref, wq_ref, k_ref, v_ref, wo_ref, o_ref, *, heads):
    x = h_ref[...]
    d = x.shape[-1]
    hd = d // heads
    q = jnp.dot(_rms(x, g_ref[...]).astype(BF16), wq_ref[...], preferred_element_type=F32)
    acc = x
    for hh in range(heads):
        hs = slice(hh * hd, (hh + 1) * hd)
        s = lax.dot_general(q[:, hs].astype(BF16), k_ref[:, hs].astype(BF16),
                            (((1,), (1,)), ((), ())), preferred_element_type=F32) * (hd ** -0.5)
        p = _softmax(s).astype(BF16)
        oh = jnp.dot(p, v_ref[:, hs].astype(BF16), preferred_element_type=F32)
        acc = acc + jnp.dot(oh.astype(BF16), wo_ref[hs, :], preferred_element_type=F32)
    o_ref[...] = acc


def _attn_prompt(h, batch, seq, norm_g, w_q, mk, mv, w_o, layer, heads):
    n, d = h.shape
    n_mem = mk.shape[1]
    tq = _tile(seq, ROW_TM)
    nt = seq // tq
    row = lambda b, t: (b * nt + t, 0)
    return pl.pallas_call(
        functools.partial(_attn_prompt_kernel, heads=heads),
        out_shape=jax.ShapeDtypeStruct((n, d), F32),
        grid=(batch, nt),
        in_specs=[
            pl.BlockSpec((tq, d), row),
            pl.BlockSpec((None, 1, d), lambda b, t: (layer, 0, 0)),
            _resident((None, d, d), lambda b, t: (layer, 0, 0)),
            pl.BlockSpec((None, n_mem, d), lambda b, t: (b, 0, 0)),
            pl.BlockSpec((None, n_mem, d), lambda b, t: (b, 0, 0)),
            _resident((None, d, d), lambda b, t: (layer, 0, 0)),
        ],
        out_specs=pl.BlockSpec((tq, d), row),
        compiler_params=_params("parallel", "parallel"),
        name="attn_prompt",
    )(h, norm_g, w_q, mk, mv, w_o)


def _attn_sample_kernel(q_ref, k_ref, v_ref, o_ref):
    bb, t, _ = q_ref.shape
    _, n_mem, heads, hd = k_ref.shape
    q = q_ref[...]
    qh = jnp.concatenate([q[:, :, h * hd:(h + 1) * hd] for h in range(heads)], axis=1).astype(BF16)
    k = k_ref[...].reshape(bb, n_mem * heads, hd).astype(BF16)
    v = v_ref[...].reshape(bb, n_mem * heads, hd).astype(BF16)
    s = jnp.einsum("bqd,bkd->bqk", qh, k, preferred_element_type=F32) * (hd ** -0.5)
    q_head = lax.broadcasted_iota(jnp.int32, s.shape, 1) // t
    k_head = lax.broadcasted_iota(jnp.int32, s.shape, 2) % heads
    p = _softmax(jnp.where(q_head == k_head, s, -jnp.inf)).astype(BF16)
    o = jnp.einsum("bqk,bkd->bqd", p, v, preferred_element_type=F32)
    for h in range(heads):
        o_ref[:, :, h * hd:(h + 1) * hd] = o[:, h * t:(h + 1) * t, :]


def _attn_sample(q, cache_k, cache_v, layer):
    bs, t, d = q.shape
    _, _, n_mem, heads, hd = cache_k.shape
    bb = _tile(bs, SAMPLE_ATTN_BB)
    kv = pl.BlockSpec((None, bb, n_mem, heads, hd), lambda i: (layer, i, 0, 0, 0))
    return pl.pallas_call(
        _attn_sample_kernel,
        out_shape=jax.ShapeDtypeStruct((bs, t, d), F32),
        grid=(bs // bb,),
        in_specs=[pl.BlockSpec((bb, t, d), lambda i: (i, 0, 0)), kv, kv],
        out_specs=pl.BlockSpec((bb, t, d), lambda i: (i, 0, 0)),
        compiler_params=_params("parallel"),
        name="attn_sample",
    )(q, cache_k, cache_v)


def kernel(x_prompt, x_sample, state_pool, state_conv, cache_mem_k, cache_mem_v, mem_prompt, ffn1_norm, ffn1_w_gate, ffn1_w_up, ffn1_w_down, mix_norm, w_in, pool_w, pool_scale, conv_w, conv_b, conv_norm_g, conv_norm_b, w_out, xattn_norm, mem_norm, w_q, w_mk, w_mv, w_o, ffn2_norm, ffn2_w_gate, ffn2_w_up, ffn2_w_down, final_norm):
    batch, seq, d = x_prompt.shape
    bs, steps, _ = x_sample.shape
    depth = state_pool.shape[0]
    pool_hist, d_pool = state_pool.shape[2:]
    conv_hist, d_conv = state_conv.shape[2:]
    n_mem, heads, head_dim = cache_mem_k.shape[2:]

    bf = lambda w: w.astype(BF16)
    ffn_w = ((bf(ffn1_w_gate), bf(ffn1_w_up), bf(ffn1_w_down)), (bf(ffn2_w_gate), bf(ffn2_w_up), bf(ffn2_w_down)))
    w_in_b, pool_w_b, w_out_b = bf(w_in), bf(pool_w), bf(w_out)
    w_q_b, w_mk_b, w_mv_b, w_o_b = bf(w_q), bf(w_mk), bf(w_mv), bf(w_o)

    vec = lambda v: v.reshape(v.shape[0], 1, v.shape[1])
    ffn_norms = (vec(ffn1_norm), vec(ffn2_norm))
    mix_norm_v, xattn_norm_v, mem_norm_v = vec(mix_norm), vec(xattn_norm), vec(mem_norm)
    pool_scale_v, conv_b_v, norm_g_v, norm_b_v = vec(pool_scale), vec(conv_b), vec(conv_norm_g), vec(conv_norm_b)
    final_g = final_norm.reshape(1, d)
    conv_head_dim = d_conv // CONV_HEADS
    conv_wb = jnp.broadcast_to(conv_w.reshape(depth, conv_hist + 1, CONV_HEADS, 1, conv_head_dim),
                               (depth, conv_hist + 1, CONV_HEADS, SUBLANE, conv_head_dim))

    mem = mem_prompt.reshape(batch * n_mem, d)

    hp = x_prompt.reshape(batch * seq, d)
    hs = x_sample.reshape(bs * steps, d)
    pool_p, conv_p, mk_p, mv_p, pool_s, conv_s = [], [], [], [], [], []

    for l in range(depth):
        last = l == depth - 1
        hp = _ffn(hp, ffn_norms[0], *ffn_w[0], l)
        hs = _ffn(hs, ffn_norms[0], *ffn_w[0], l)

        p, glu = _proj_in(hp, mix_norm_v, w_in_b, l, d_pool, d_conv)
        pool_p.append(p.reshape(batch, seq, d_pool)[:, seq - pool_hist:])
        conv_p.append(glu.reshape(batch, seq, d_conv)[:, seq - conv_hist:])
        hp = _mix_prompt(p, glu, hp, batch, seq, pool_w_b, pool_scale_v, conv_wb, conv_b_v, norm_g_v, norm_b_v,
                         w_out_b, l, pool_hist, conv_hist)

        p, glu = _proj_in(hs, mix_norm_v, w_in_b, l, d_pool, d_conv)
        p_ext = jnp.concatenate([state_pool[l], p.reshape(bs, steps, d_pool)], axis=1)
        g_ext = jnp.concatenate([state_conv[l], glu.reshape(bs, steps, d_conv)], axis=1)
        pool_s.append(p_ext[:, -pool_hist:])
        conv_s.append(g_ext[:, -conv_hist:])
        po, co = _mix_sample(p_ext.transpose(1, 0, 2), g_ext.transpose(1, 0, 2), pool_w_b, pool_scale_v, conv_w,
                             conv_b_v, norm_g_v, norm_b_v, l, steps, pool_hist, conv_hist)
        mixed = jnp.concatenate([po, co], axis=-1).transpose(1, 0, 2).reshape(bs * steps, d_pool + d_conv)
        hs = _matmul_res(hs, mixed, w_out_b, l)

        mk = _rms_proj(mem, mem_norm_v, w_mk_b, l)
        mv = _rms_proj(mem, mem_norm_v, w_mv_b, l)
        mk_p.append(mk.reshape(batch, n_mem, heads, head_dim))
        mv_p.append(mv.reshape(batch, n_mem, heads, head_dim))
        hp = _attn_prompt(hp, batch, seq, xattn_norm_v, w_q_b, mk.reshape(batch, n_mem, d),
                          mv.reshape(batch, n_mem, d), w_o_b, l, heads)
        q = _rms_proj(hs, xattn_norm_v, w_q_b, l)
        o = _attn_sample(q.reshape(bs, steps, d), cache_mem_k, cache_mem_v, l)
        hs = _matmul_res(hs, o.reshape(bs * steps, d), w_o_b, l)

        hp = _ffn(hp, ffn_norms[1], *ffn_w[1], l, final_g if last else None)
        hs = _ffn(hs, ffn_norms[1], *ffn_w[1], l, final_g if last else None)

    return (hp.reshape(batch, seq, d), hs.reshape(bs, steps, d), jnp.stack(pool_p), jnp.stack(conv_p),
            jnp.stack(mk_p), jnp.stack(mv_p), jnp.stack(pool_s), jnp.stack(conv_s))
```

```python
import functools

import jax
import jax.numpy as jnp
from jax import lax
from jax.experimental import pallas as pl
from jax.experimental.pallas import tpu as pltpu

F32 = jnp.float32
BF16 = jnp.bfloat16

EPS = 1e-6
PAST_LEN = 16384
POOL_WINDOWS = (2, 4, 8, 16)
CONV_HEADS = 8

LANE = 128
SUBLANE = 8
VMEM_LIMIT = 56 * 1024 * 1024

FFN_TM = 1024
FFN_TF = 1024
FFN_SUB = 256
FFN_EMIT_TF = 256
FFN_VMEM_LIMIT = 60 * 1024 * 1024
ROW_TM = 512
MIX_TS = 512
CONV_CHUNK = 64
SAMPLE_ATTN_BB = 4


def _params(*sem, vmem_limit=VMEM_LIMIT):
    return pltpu.CompilerParams(dimension_semantics=sem, vmem_limit_bytes=vmem_limit)


def _resident(block_shape, index_map):
    return pl.BlockSpec(block_shape, index_map, pipeline_mode=pl.Buffered(1))


def _rms(x, g):
    return x * lax.rsqrt(jnp.mean(x * x, axis=-1, keepdims=True) + EPS) * g


def _silu(x):
    return x * jax.nn.sigmoid(x)


def _tile(n, t):
    t = min(n, t)
    assert n % t == 0, (n, t)
    return t


def _ffn_kernel(x_ref, g_ref, wg_ref, wu_ref, wd_ref, fg_ref, o_ref, *rest, final_norm, d_ff, sub, emit):
    if emit:
        wg_o, wu_o, wd_o, xn_ref = rest
    else:
        (xn_ref,) = rest
    j = pl.program_id(1)
    tf = wg_ref.shape[-1]
    nj = pl.cdiv(d_ff, tf)

    @pl.when(j == 0)
    def _():
        xn_ref[...] = _rms(x_ref[...], g_ref[...]).astype(BF16)
        o_ref[...] = jnp.zeros_like(o_ref)

    if emit:
        wg_o[...] = wg_ref[0].astype(BF16)
        wu_o[...] = wu_ref[0].astype(BF16)
        wd_o[...] = wd_ref[0].astype(BF16)
        wg, wu, wd = wg_o, wu_o, wd_o
    else:
        wg, wu, wd = wg_ref, wu_ref, wd_ref

    def columns(lo, hi):
        xn = xn_ref[...]
        for c0 in range(lo, hi, sub):
            c1 = min(c0 + sub, hi)
            gate = jnp.dot(xn, wg[:, c0:c1], preferred_element_type=F32)
            up = jnp.dot(xn, wu[:, c0:c1], preferred_element_type=F32)
            hmid = (_silu(gate) * up).astype(BF16)
            o_ref[...] += jnp.dot(hmid, wd[c0:c1, :], preferred_element_type=F32)

    done = nj * tf - d_ff
    if done == 0:
        columns(0, tf)
    else:
        pl.when(j < nj - 1)(lambda: columns(0, tf))
        pl.when(j == nj - 1)(lambda: columns(done, tf))

    @pl.when(j == nj - 1)
    def _():
        y = x_ref[...] + 0.5 * o_ref[...]
        if final_norm:
            y = _rms(y, fg_ref[...])
        o_ref[...] = y


def _ffn(x, norm_g, wg, wu, wd, layer, final_g=None):
    n, d = x.shape
    d_ff = wg.shape[-1]
    emit = wg.ndim == 3
    tm = _tile(n, FFN_TM)
    tf, sub = (FFN_EMIT_TF, FFN_EMIT_TF) if emit else (FFN_TF, FFN_SUB)
    tf = min(tf, d_ff)
    assert tf % LANE == 0 and d_ff % LANE == 0
    final_norm = final_g is not None
    if final_g is None:
        final_g = jnp.ones((1, d), F32)
    start = lambda j: pl.multiple_of(jnp.minimum(j * tf, d_ff - tf), LANE)
    el = pl.Element
    bf16_specs = [
        pl.BlockSpec((el(d), el(tf)), lambda i, j: (0, start(j))),
        pl.BlockSpec((el(d), el(tf)), lambda i, j: (0, start(j))),
        pl.BlockSpec((el(tf), el(d)), lambda i, j: (start(j), 0)),
    ]
    f32_specs = [
        pl.BlockSpec((el(1), el(d), el(tf)), lambda i, j: (layer, 0, start(j))),
        pl.BlockSpec((el(1), el(d), el(tf)), lambda i, j: (layer, 0, start(j))),
        pl.BlockSpec((el(1), el(tf), el(d)), lambda i, j: (layer, start(j), 0)),
    ]
    out_shape = [jax.ShapeDtypeStruct((n, d), F32)]
    out_specs = [pl.BlockSpec((tm, d), lambda i, j: (i, 0))]
    if emit:
        assert n == tm, "the emitted casts are written by a single row tile"
        out_shape += [jax.ShapeDtypeStruct((d, d_ff), BF16), jax.ShapeDtypeStruct((d, d_ff), BF16),
                      jax.ShapeDtypeStruct((d_ff, d), BF16)]
        out_specs += bf16_specs
    outs = pl.pallas_call(
        functools.partial(_ffn_kernel, final_norm=final_norm, d_ff=d_ff, sub=sub, emit=emit),
        out_shape=out_shape,
        grid=(n // tm, pl.cdiv(d_ff, tf)),
        in_specs=[
            _resident((tm, d), lambda i, j: (i, 0)),
            pl.BlockSpec((None, 1, d), lambda i, j: (layer, 0, 0)),
            *(f32_specs if emit else bf16_specs),
            pl.BlockSpec((1, d), lambda i, j: (0, 0)),
        ],
        out_specs=out_specs,
        scratch_shapes=[pltpu.VMEM((tm, d), BF16)],
        compiler_params=_params("parallel", "arbitrary", vmem_limit=FFN_VMEM_LIMIT),
        name="ffn_emit" if emit else "ffn",
    )(x, norm_g, wg, wu, wd, final_g)
    return outs if emit else outs[0]


def _proj_in_kernel(x_ref, g_ref, w_ref, p_ref, glu_ref, *, d_pool, d_conv):
    xn = _rms(x_ref[...], g_ref[...]).astype(BF16)
    z = jnp.dot(xn, w_ref[...], preferred_element_type=F32)
    p_ref[...] = z[:, :d_pool]
    glu_ref[...] = z[:, d_pool:d_pool + d_conv] * jax.nn.sigmoid(z[:, d_pool + d_conv:])


def _proj_in(x, norm_g, w_in, layer, d_pool, d_conv):
    n, d = x.shape
    tm = _tile(n, ROW_TM)
    dz = w_in.shape[-1]
    return pl.pallas_call(
        functools.partial(_proj_in_kernel, d_pool=d_pool, d_conv=d_conv),
        out_shape=(jax.ShapeDtypeStruct((n, d_pool), F32), jax.ShapeDtypeStruct((n, d_conv), F32)),
        grid=(n // tm,),
        in_specs=[
            pl.BlockSpec((tm, d), lambda i: (i, 0)),
            pl.BlockSpec((None, 1, d), lambda i: (layer, 0, 0)),
            _resident((None, d, dz), lambda i: (layer, 0, 0)),
        ],
        out_specs=(pl.BlockSpec((tm, d_pool), lambda i: (i, 0)),
                   pl.BlockSpec((tm, d_conv), lambda i: (i, 0))),
        compiler_params=_params("parallel"),
        name="proj_in",
    )(x, norm_g, w_in)


def _rms_proj_kernel(x_ref, g_ref, w_ref, o_ref):
    xn = _rms(x_ref[...], g_ref[...]).astype(BF16)
    o_ref[...] = jnp.dot(xn, w_ref[...].astype(BF16), preferred_element_type=F32)


def _rms_proj(x, norm_g, w, layer):
    n, d = x.shape
    dn = w.shape[-1]
    tm = _tile(n, ROW_TM)
    return pl.pallas_call(
        _rms_proj_kernel,
        out_shape=jax.ShapeDtypeStruct((n, dn), F32),
        grid=(n // tm,),
        in_specs=[
            pl.BlockSpec((tm, d), lambda i: (i, 0)),
            pl.BlockSpec((None, 1, d), lambda i: (layer, 0, 0)),
            _resident((None, d, dn), lambda i: (layer, 0, 0)),
        ],
        out_specs=pl.BlockSpec((tm, dn), lambda i: (i, 0)),
        compiler_params=_params("parallel"),
        name="rms_proj",
    )(x, norm_g, w)


def _matmul_res_kernel(res_ref, x_ref, w_ref, o_ref):
    o_ref[...] = res_ref[...] + jnp.dot(x_ref[...].astype(BF16), w_ref[...], preferred_element_type=F32)


def _matmul_res(res, x, w, layer):
    n, d = res.shape
    dk = x.shape[-1]
    tm = _tile(n, ROW_TM)
    return pl.pallas_call(
        _matmul_res_kernel,
        out_shape=jax.ShapeDtypeStruct((n, d), F32),
        grid=(n // tm,),
        in_specs=[
            pl.BlockSpec((tm, d), lambda i: (i, 0)),
            pl.BlockSpec((tm, dk), lambda i: (i, 0)),
            _resident((None, dk, d), lambda i: (layer, 0, 0)),
        ],
        out_specs=pl.BlockSpec((tm, d), lambda i: (i, 0)),
        compiler_params=_params("parallel"),
        name="matmul_res",
    )(res, x, w)


def _head_norm_swish(y, g, b):
    yc = y - jnp.mean(y, axis=-1, keepdims=True)
    var = jnp.mean(yc * yc, axis=-1, keepdims=True)
    return _silu(yc * lax.rsqrt(var + EPS) * g + b)


def _mix_prompt_kernel(p_ref, glu_ref, h_ref, pw_ref, ps_ref, cw_ref, cb_ref, ng_ref, nb_ref, wo_ref,
                       o_ref, pext, gs, ys, pool_out, *, ts, conv_hist):
    t = pl.program_id(1)
    d_pool = p_ref.shape[-1]
    pool_group = d_pool // len(POOL_WINDOWS)
    heads, _, hd = gs.shape
    conv_width = conv_hist + 1
    ph = pext.shape[0] - ts
    gh = gs.shape[1] - ts

    @pl.when(t == 0)
    def _():
        pext[0:ph, :] = jnp.zeros((ph, d_pool), F32)
        gs[:, 0:gh, :] = jnp.zeros((heads, gh, hd), F32)

    pext[ph:ph + ts, :] = p_ref[...]
    for c in range(heads):
        gs[c, gh:gh + ts, :] = glu_ref[:, c * hd:(c + 1) * hd]

    pos = t * ts + lax.broadcasted_iota(jnp.int32, (ts, 1), 0)
    for g, w in enumerate(POOL_WINDOWS):
        sl = slice(g * pool_group, (g + 1) * pool_group)
        cur = pext[ph:ph + ts, sl]
        acc = cur
        for k in range(1, w):
            acc = acc + pext[ph - k:ph - k + ts, sl]
        cnt = jnp.minimum(pos + 1, w).astype(F32)
        dlt = (acc / cnt - cur).astype(BF16)
        y = jnp.dot(dlt, pw_ref[g], preferred_element_type=F32) * ps_ref[:, sl]
        pool_out[:, sl] = y.astype(BF16)

    groups = CONV_CHUNK // (2 * SUBLANE)

    def conv_head(c):
        hs = slice(c * hd, (c + 1) * hd)
        bias = jnp.broadcast_to(cb_ref[:, hs], (SUBLANE, hd))
        for r0 in range(0, ts, CONV_CHUNK):
            even = [bias] * groups
            odd = [bias] * groups
            for j in range(conv_width + 1):
                rows = [gs[c, pl.ds(r0 + q * 2 * SUBLANE + (gh - conv_hist) + j, SUBLANE, stride=2), :]
                        for q in range(groups)]
                if j < conv_width:
                    wk = cw_ref[j, c]
                    even = [a + x * wk for a, x in zip(even, rows)]
                if j >= 1:
                    wk = cw_ref[j - 1, c]
                    odd = [a + x * wk for a, x in zip(odd, rows)]
            for q in range(groups):
                for phase, acc in enumerate((even[q], odd[q])):
                    y = _head_norm_swish(acc, ng_ref[:, hs], nb_ref[:, hs])
                    ys[c, pl.ds(r0 + q * 2 * SUBLANE + phase, SUBLANE, stride=2), :] = y

    mixed = jnp.dot(pool_out[...], wo_ref[0:d_pool, :], preferred_element_type=F32)
    for c0 in range(0, heads, 2):
        conv_head(c0)
        conv_head(c0 + 1)
        pair = jnp.concatenate([ys[c0], ys[c0 + 1]], axis=1).astype(BF16)
        rows = slice(d_pool + c0 * hd, d_pool + (c0 + 2) * hd)
        mixed = mixed + jnp.dot(pair, wo_ref[rows, :], preferred_element_type=F32)
    o_ref[...] = h_ref[...] + mixed

    pext[0:ph, :] = pext[ts:ts + ph, :]
    for c in range(heads):
        gs[c, 0:gh, :] = gs[c, ts:ts + gh, :]


def _mix_prompt(p, glu, h, batch, seq, pool_w, pool_scale, conv_wb, conv_b, norm_g, norm_b, w_out, layer,
                pool_hist, conv_hist):
    n, d = h.shape
    d_pool, d_conv = p.shape[-1], glu.shape[-1]
    heads, hd = conv_wb.shape[2], conv_wb.shape[4]
    assert heads * hd == d_conv and hd == LANE
    ts = _tile(seq, MIX_TS)
    assert ts % CONV_CHUNK == 0
    nt = seq // ts
    ph = -(-pool_hist // SUBLANE) * SUBLANE
    gh = -(-conv_hist // SUBLANE) * SUBLANE
    row = lambda b, t: (b * nt + t, 0)
    vec = lambda b, t: (layer, 0, 0)
    return pl.pallas_call(
        functools.partial(_mix_prompt_kernel, ts=ts, conv_hist=conv_hist),
        out_shape=jax.ShapeDtypeStruct((n, d), F32),
        grid=(batch, nt),
        in_specs=[
            pl.BlockSpec((ts, d_pool), row),
            pl.BlockSpec((ts, d_conv), row),
            pl.BlockSpec((ts, d), row),
            _resident((None,) + pool_w.shape[1:], lambda b, t: (layer, 0, 0, 0)),
            pl.BlockSpec((None, 1, d_pool), vec),
            _resident((None,) + conv_wb.shape[1:], lambda b, t: (layer, 0, 0, 0, 0)),
            pl.BlockSpec((None, 1, d_conv), vec),
            pl.BlockSpec((None, 1, d_conv), vec),
            pl.BlockSpec((None, 1, d_conv), vec),
            _resident((None, d_pool + d_conv, d), vec),
        ],
        out_specs=pl.BlockSpec((ts, d), row),
        scratch_shapes=[
            pltpu.VMEM((ph + ts, d_pool), F32),
            pltpu.VMEM((heads, gh + ts, hd), F32),
            pltpu.VMEM((heads, ts, hd), F32),
            pltpu.VMEM((ts, d_pool), BF16),
        ],
        compiler_params=_params("arbitrary", "arbitrary"),
        name="mix_prompt",
    )(p, glu, h, pool_w, pool_scale, conv_wb, conv_b, norm_g, norm_b, w_out)


def _mix_sample_kernel(pe_ref, ge_ref, pw_ref, ps_ref, cw_ref, cb_ref, ng_ref, nb_ref,
                       po_ref, co_ref, dbuf, *, steps, pool_hist, conv_hist):
    g = pl.program_id(0)
    bs, cb = pe_ref.shape[1], pe_ref.shape[2]
    head_dim = ng_ref.shape[-1] * len(POOL_WINDOWS) // CONV_HEADS
    conv_width = conv_hist + 1

    for t in range(steps):
        cur = pe_ref[pool_hist + t]
        acc = cur
        dlt = jnp.zeros_like(cur)
        k = 1
        for gi, w in enumerate(POOL_WINDOWS):
            while k < w:
                acc = acc + pe_ref[pool_hist + t - k]
                k += 1
            cnt = float(min(PAST_LEN + t + 1, w))
            dlt = jnp.where(g == gi, acc / cnt - cur, dlt)
        dbuf[t] = dlt.astype(BF16)

        y = jnp.broadcast_to(cb_ref[...], (bs, cb))
        for k in range(conv_width):
            y = y + ge_ref[t + k] * cw_ref[k:k + 1, :]
        for hh in range(cb // head_dim):
            hs = slice(hh * head_dim, (hh + 1) * head_dim)
            co_ref[t, :, hs] = _head_norm_swish(y[:, hs], ng_ref[:, hs], nb_ref[:, hs]).astype(BF16)

    dall = dbuf[...].reshape(steps * bs, cb)
    y = jnp.dot(dall, pw_ref[...], preferred_element_type=F32) * ps_ref[...]
    po_ref[...] = y.reshape(steps, bs, cb).astype(BF16)


def _mix_sample(pext_t, gext_t, pool_w, pool_scale, conv_w, conv_b, norm_g, norm_b, layer, steps,
                pool_hist, conv_hist):
    rp, bs, d_pool = pext_t.shape
    rg, _, d_conv = gext_t.shape
    ng = len(POOL_WINDOWS)
    cb = d_pool // ng
    assert d_conv // ng == cb and cb % (d_conv // CONV_HEADS) == 0
    vec = lambda g: (layer, 0, g)
    return pl.pallas_call(
        functools.partial(_mix_sample_kernel, steps=steps, pool_hist=pool_hist, conv_hist=conv_hist),
        out_shape=(jax.ShapeDtypeStruct((steps, bs, d_pool), BF16),
                   jax.ShapeDtypeStruct((steps, bs, d_conv), BF16)),
        grid=(ng,),
        in_specs=[
            pl.BlockSpec((rp, bs, cb), lambda g: (0, 0, g)),
            pl.BlockSpec((rg, bs, cb), lambda g: (0, 0, g)),
            pl.BlockSpec((None, None, cb, cb), lambda g: (layer, g, 0, 0)),
            pl.BlockSpec((None, 1, cb), vec),
            pl.BlockSpec((None, conv_hist + 1, cb), vec),
            pl.BlockSpec((None, 1, cb), vec),
            pl.BlockSpec((None, 1, cb), vec),
            pl.BlockSpec((None, 1, cb), vec),
        ],
        out_specs=(pl.BlockSpec((steps, bs, cb), lambda g: (0, 0, g)),
                   pl.BlockSpec((steps, bs, cb), lambda g: (0, 0, g))),
        scratch_shapes=[pltpu.VMEM((steps, bs, cb), BF16)],
        compiler_params=_params("parallel"),
        name="mix_sample",
    )(pext_t, gext_t, pool_w, pool_scale, conv_w, conv_b, norm_g, norm_b)


def _softmax(s):
    e = jnp.exp(s - jnp.max(s, axis=-1, keepdims=True))
    return e / jnp.sum(e, axis=-1, keepdims=True)


def _attn_prompt_kernel(h_ref, g_ref, wq_ref, k_ref, v_ref, wo_ref, o_ref, *, heads):
    x = h_ref[...]
    d = x.shape[-1]
    hd = d // heads
    q = jnp.dot(_rms(x, g_ref[...]).astype(BF16), wq_ref[...], preferred_element_type=F32)
    acc = x
    for hh in range(heads):
        hs = slice(hh * hd, (hh + 1) * hd)
        s = lax.dot_general(q[:, hs].astype(BF16), k_ref[:, hs].astype(BF16),
                            (((1,), (1,)), ((), ())), preferred_element_type=F32) * (hd ** -0.5)
        p = _softmax(s).astype(BF16)
        oh = jnp.dot(p, v_ref[:, hs].astype(BF16), preferred_element_type=F32)
        acc = acc + jnp.dot(oh.astype(BF16), wo_ref[hs, :], preferred_element_type=F32)
    o_ref[...] = acc


def _attn_prompt(h, batch, seq, norm_g, w_q, mk, mv, w_o, layer, heads):
    n, d = h.shape
    n_mem = mk.shape[1]
    tq = _tile(seq, ROW_TM)
    nt = seq // tq
    row = lambda b, t: (b * nt + t, 0)
    return pl.pallas_call(
        functools.partial(_attn_prompt_kernel, heads=heads),
        out_shape=jax.ShapeDtypeStruct((n, d), F32),
        grid=(batch, nt),
        in_specs=[
            pl.BlockSpec((tq, d), row),
            pl.BlockSpec((None, 1, d), lambda b, t: (layer, 0, 0)),
            _resident((None, d, d), lambda b, t: (layer, 0, 0)),
            pl.BlockSpec((None, n_mem, d), lambda b, t: (b, 0, 0)),
            pl.BlockSpec((None, n_mem, d), lambda b, t: (b, 0, 0)),
            _resident((None, d, d), lambda b, t: (layer, 0, 0)),
        ],
        out_specs=pl.BlockSpec((tq, d), row),
        compiler_params=_params("parallel", "parallel"),
        name="attn_prompt",
    )(h, norm_g, w_q, mk, mv, w_o)


def _attn_sample_kernel(q_ref, k_ref, v_ref, o_ref):
    bb, t, _ = q_ref.shape
    _, n_mem, heads, hd = k_ref.shape
    q = q_ref[...]
    qh = jnp.concatenate([q[:, :, h * hd:(h + 1) * hd] for h in range(heads)], axis=1).astype(BF16)
    k = k_ref[...].reshape(bb, n_mem * heads, hd).astype(BF16)
    v = v_ref[...].reshape(bb, n_mem * heads, hd).astype(BF16)
    s = jnp.einsum("bqd,bkd->bqk", qh, k, preferred_element_type=F32) * (hd ** -0.5)
    q_head = lax.broadcasted_iota(jnp.int32, s.shape, 1) // t
    k_head = lax.broadcasted_iota(jnp.int32, s.shape, 2) % heads
    p = _softmax(jnp.where(q_head == k_head, s, -jnp.inf)).astype(BF16)
    o = jnp.einsum("bqk,bkd->bqd", p, v, preferred_element_type=F32)
    for h in range(heads):
        o_ref[:, :, h * hd:(h + 1) * hd] = o[:, h * t:(h + 1) * t, :]


def _attn_sample(q, cache_k, cache_v, layer):
    bs, t, d = q.shape
    _, _, n_mem, heads, hd = cache_k.shape
    bb = _tile(bs, SAMPLE_ATTN_BB)
    kv = pl.BlockSpec((None, bb, n_mem, heads, hd), lambda i: (layer, i, 0, 0, 0))
    return pl.pallas_call(
        _attn_sample_kernel,
        out_shape=jax.ShapeDtypeStruct((bs, t, d), F32),
        grid=(bs // bb,),
        in_specs=[pl.BlockSpec((bb, t, d), lambda i: (i, 0, 0)), kv, kv],
        out_specs=pl.BlockSpec((bb, t, d), lambda i: (i, 0, 0)),
        compiler_params=_params("parallel"),
        name="attn_sample",
    )(q, cache_k, cache_v)


def kernel(x_prompt, x_sample, state_pool, state_conv, cache_mem_k, cache_mem_v, mem_prompt, ffn1_norm, ffn1_w_gate, ffn1_w_up, ffn1_w_down, mix_norm, w_in, pool_w, pool_scale, conv_w, conv_b, conv_norm_g, conv_norm_b, w_out, xattn_norm, mem_norm, w_q, w_mk, w_mv, w_o, ffn2_norm, ffn2_w_gate, ffn2_w_up, ffn2_w_down, final_norm):
    batch, seq, d = x_prompt.shape
    bs, steps, _ = x_sample.shape
    depth = state_pool.shape[0]
    pool_hist, d_pool = state_pool.shape[2:]
    conv_hist, d_conv = state_conv.shape[2:]
    n_mem, heads, head_dim = cache_mem_k.shape[2:]

    bf = lambda w: w.astype(BF16)
    ffn_w = ((ffn1_w_gate, ffn1_w_up, ffn1_w_down), (ffn2_w_gate, ffn2_w_up, ffn2_w_down))
    w_in_b, pool_w_b, w_out_b = bf(w_in), bf(pool_w), bf(w_out)
    w_q_b, w_o_b = bf(w_q), bf(w_o)

    vec = lambda v: v.reshape(v.shape[0], 1, v.shape[1])
    ffn_norms = (vec(ffn1_norm), vec(ffn2_norm))
    mix_norm_v, xattn_norm_v, mem_norm_v = vec(mix_norm), vec(xattn_norm), vec(mem_norm)
    pool_scale_v, conv_b_v, norm_g_v, norm_b_v = vec(pool_scale), vec(conv_b), vec(conv_norm_g), vec(conv_norm_b)
    final_g = final_norm.reshape(1, d)
    conv_head_dim = d_conv // CONV_HEADS
    conv_wb = jnp.broadcast_to(conv_w.reshape(depth, conv_hist + 1, CONV_HEADS, 1, conv_head_dim),
                               (depth, conv_hist + 1, CONV_HEADS, SUBLANE, conv_head_dim))

    mem = mem_prompt.reshape(batch * n_mem, d)

    hp = x_prompt.reshape(batch * seq, d)
    hs = x_sample.reshape(bs * steps, d)
    pool_p, conv_p, mk_p, mv_p, pool_s, conv_s = [], [], [], [], [], []

    for l in range(depth):
        last = l == depth - 1
        hs, *ffn_wb = _ffn(hs, ffn_norms[0], *ffn_w[0], l)
        hp = _ffn(hp, ffn_norms[0], *ffn_wb, l)

        p, glu = _proj_in(hp, mix_norm_v, w_in_b, l, d_pool, d_conv)
        pool_p.append(p.reshape(batch, seq, d_pool)[:, seq - pool_hist:])
        conv_p.append(glu.reshape(batch, seq, d_conv)[:, seq - conv_hist:])
        hp = _mix_prompt(p, glu, hp, batch, seq, pool_w_b, pool_scale_v, conv_wb, conv_b_v, norm_g_v, norm_b_v,
                         w_out_b, l, pool_hist, conv_hist)

        p, glu = _proj_in(hs, mix_norm_v, w_in_b, l, d_pool, d_conv)
        p_ext = jnp.concatenate([state_pool[l], p.reshape(bs, steps, d_pool)], axis=1)
        g_ext = jnp.concatenate([state_conv[l], glu.reshape(bs, steps, d_conv)], axis=1)
        pool_s.append(p_ext[:, -pool_hist:])
        conv_s.append(g_ext[:, -conv_hist:])
        po, co = _mix_sample(p_ext.transpose(1, 0, 2), g_ext.transpose(1, 0, 2), pool_w_b, pool_scale_v, conv_w,
                             conv_b_v, norm_g_v, norm_b_v, l, steps, pool_hist, conv_hist)
        mixed = jnp.concatenate([po, co], axis=-1).transpose(1, 0, 2).reshape(bs * steps, d_pool + d_conv)
        hs = _matmul_res(hs, mixed, w_out_b, l)

        mk = _rms_proj(mem, mem_norm_v, w_mk, l)
        mv = _rms_proj(mem, mem_norm_v, w_mv, l)
        mk_p.append(mk.reshape(batch, n_mem, heads, head_dim))
        mv_p.append(mv.reshape(batch, n_mem, heads, head_dim))
        hp = _attn_prompt(hp, batch, seq, xattn_norm_v, w_q_b, mk.reshape(batch, n_mem, d),
                          mv.reshape(batch, n_mem, d), w_o_b, l, heads)
        q = _rms_proj(hs, xattn_norm_v, w_q_b, l)
        o = _attn_sample(q.reshape(bs, steps, d), cache_mem_k, cache_mem_v, l)
        hs = _matmul_res(hs, o.reshape(bs * steps, d), w_o_b, l)

        hs, *ffn_wb = _ffn(hs, ffn_norms[1], *ffn_w[1], l, final_g if last else None)
        hp = _ffn(hp, ffn_norms[1], *ffn_wb, l, final_g if last else None)

    return (hp.reshape(batch, seq, d), hs.reshape(bs, steps, d), jnp.stack(pool_p), jnp.stack(conv_p),
            jnp.stack(mk_p), jnp.stack(mv_p), jnp.stack(pool_s), jnp.stack(conv_s))
```

```python
import functools

import jax
import jax.numpy as jnp
from jax import lax
from jax.experimental import pallas as pl
from jax.experimental.pallas import tpu as pltpu

F32 = jnp.float32
BF16 = jnp.bfloat16

EPS = 1e-6
PAST_LEN = 16384
POOL_WINDOWS = (2, 4, 8, 16)
CONV_HEADS = 8

LANE = 128
SUBLANE = 8
VMEM_LIMIT = 56 * 1024 * 1024

FFN_TM = 1024
FFN_TF = 1024
FFN_SUB = 256
FFN_EMIT_TF = 256
FFN_VMEM_LIMIT = 60 * 1024 * 1024
MIX_VMEM_LIMIT = 60 * 1024 * 1024
ROW_TM = 512
MIX_TS = 512
CONV_CHUNK = 64
SAMPLE_ATTN_BB = 4


def _params(*sem, vmem_limit=VMEM_LIMIT):
    return pltpu.CompilerParams(dimension_semantics=sem, vmem_limit_bytes=vmem_limit)


def _resident(block_shape, index_map):
    return pl.BlockSpec(block_shape, index_map, pipeline_mode=pl.Buffered(1))


def _rms(x, g):
    return x * lax.rsqrt(jnp.mean(x * x, axis=-1, keepdims=True) + EPS) * g


def _silu(x):
    return x * jax.nn.sigmoid(x)


def _tile(n, t):
    t = min(n, t)
    assert n % t == 0, (n, t)
    return t


def _ffn_kernel(x_ref, g_ref, wg_ref, wu_ref, wd_ref, fg_ref, o_ref, *rest, final_norm, d_ff, sub, emit):
    if emit:
        wg_o, wu_o, wd_o, xn_ref = rest
    else:
        (xn_ref,) = rest
    j = pl.program_id(1)
    tf = wg_ref.shape[-1]
    nj = pl.cdiv(d_ff, tf)

    @pl.when(j == 0)
    def _():
        xn_ref[...] = _rms(x_ref[...], g_ref[...]).astype(BF16)
        o_ref[...] = jnp.zeros_like(o_ref)

    def columns(lo, n):
        if emit:
            wg_o[:, 0:n] = wg_ref[0, :, lo:lo + n].astype(BF16)
            wu_o[:, 0:n] = wu_ref[0, :, lo:lo + n].astype(BF16)
            wd_o[0:n, :] = wd_ref[0, lo:lo + n, :].astype(BF16)
            wg, wu, wd, lo = wg_o, wu_o, wd_o, 0
        else:
            wg, wu, wd = wg_ref, wu_ref, wd_ref
        xn = xn_ref[...]
        for c0 in range(lo, lo + n, sub):
            c1 = min(c0 + sub, lo + n)
            gate = jnp.dot(xn, wg[:, c0:c1], preferred_element_type=F32)
            up = jnp.dot(xn, wu[:, c0:c1], preferred_element_type=F32)
            hmid = (_silu(gate) * up).astype(BF16)
            o_ref[...] += jnp.dot(hmid, wd[c0:c1, :], preferred_element_type=F32)

    done = nj * tf - d_ff
    if done == 0:
        columns(0, tf)
    else:
        pl.when(j < nj - 1)(lambda: columns(0, tf))
        pl.when(j == nj - 1)(lambda: columns(done, tf - done))

    @pl.when(j == nj - 1)
    def _():
        y = x_ref[...] + 0.5 * o_ref[...]
        if final_norm:
            y = _rms(y, fg_ref[...])
        o_ref[...] = y


def _ffn(x, norm_g, wg, wu, wd, layer, final_g=None):
    n, d = x.shape
    d_ff = wg.shape[-1]
    emit = wg.ndim == 3
    tm = _tile(n, FFN_TM)
    tf, sub = (FFN_EMIT_TF, FFN_EMIT_TF) if emit else (FFN_TF, FFN_SUB)
    tf = min(tf, d_ff)
    assert tf % LANE == 0 and d_ff % LANE == 0
    final_norm = final_g is not None
    if final_g is None:
        final_g = jnp.ones((1, d), F32)
    start = lambda j: pl.multiple_of(jnp.minimum(j * tf, d_ff - tf), LANE)
    el = pl.Element
    bf16_specs = [
        pl.BlockSpec((el(d), el(tf)), lambda i, j: (0, start(j))),
        pl.BlockSpec((el(d), el(tf)), lambda i, j: (0, start(j))),
        pl.BlockSpec((el(tf), el(d)), lambda i, j: (start(j), 0)),
    ]
    f32_specs = [
        pl.BlockSpec((el(1), el(d), el(tf)), lambda i, j: (layer, 0, start(j))),
        pl.BlockSpec((el(1), el(d), el(tf)), lambda i, j: (layer, 0, start(j))),
        pl.BlockSpec((el(1), el(tf), el(d)), lambda i, j: (layer, start(j), 0)),
    ]
    out_shape = [jax.ShapeDtypeStruct((n, d), F32)]
    out_specs = [pl.BlockSpec((tm, d), lambda i, j: (i, 0))]
    if emit:
        assert n == tm, "the emitted casts are written by a single row tile"
        out_shape += [jax.ShapeDtypeStruct((d, d_ff), BF16), jax.ShapeDtypeStruct((d, d_ff), BF16),
                      jax.ShapeDtypeStruct((d_ff, d), BF16)]
        out_specs += [pl.BlockSpec((d, tf), lambda i, j: (0, j)), pl.BlockSpec((d, tf), lambda i, j: (0, j)),
                      pl.BlockSpec((tf, d), lambda i, j: (j, 0))]
    outs = pl.pallas_call(
        functools.partial(_ffn_kernel, final_norm=final_norm, d_ff=d_ff, sub=sub, emit=emit),
        out_shape=out_shape,
        grid=(n // tm, pl.cdiv(d_ff, tf)),
        in_specs=[
            _resident((tm, d), lambda i, j: (i, 0)),
            pl.BlockSpec((None, 1, d), lambda i, j: (layer, 0, 0)),
            *(f32_specs if emit else bf16_specs),
            pl.BlockSpec((1, d), lambda i, j: (0, 0)),
        ],
        out_specs=out_specs,
        scratch_shapes=[pltpu.VMEM((tm, d), BF16)],
        compiler_params=_params("parallel", "arbitrary", vmem_limit=FFN_VMEM_LIMIT),
        name="ffn_emit" if emit else "ffn",
    )(x, norm_g, wg, wu, wd, final_g)
    return outs if emit else outs[0]


def _proj_in_kernel(x_ref, g_ref, w_ref, p_ref, glu_ref, *, d_pool, d_conv):
    xn = _rms(x_ref[...], g_ref[...]).astype(BF16)
    z = jnp.dot(xn, w_ref[...], preferred_element_type=F32)
    p_ref[...] = z[:, :d_pool]
    glu_ref[...] = z[:, d_pool:d_pool + d_conv] * jax.nn.sigmoid(z[:, d_pool + d_conv:])


def _proj_in(x, norm_g, w_in, layer, d_pool, d_conv):
    n, d = x.shape
    tm = _tile(n, ROW_TM)
    dz = w_in.shape[-1]
    return pl.pallas_call(
        functools.partial(_proj_in_kernel, d_pool=d_pool, d_conv=d_conv),
        out_shape=(jax.ShapeDtypeStruct((n, d_pool), F32), jax.ShapeDtypeStruct((n, d_conv), F32)),
        grid=(n // tm,),
        in_specs=[
            pl.BlockSpec((tm, d), lambda i: (i, 0)),
            pl.BlockSpec((None, 1, d), lambda i: (layer, 0, 0)),
            _resident((None, d, dz), lambda i: (layer, 0, 0)),
        ],
        out_specs=(pl.BlockSpec((tm, d_pool), lambda i: (i, 0)),
                   pl.BlockSpec((tm, d_conv), lambda i: (i, 0))),
        compiler_params=_params("parallel"),
        name="proj_in",
    )(x, norm_g, w_in)


def _rms_proj_kernel(x_ref, g_ref, w_ref, o_ref):
    xn = _rms(x_ref[...], g_ref[...]).astype(BF16)
    o_ref[...] = jnp.dot(xn, w_ref[...].astype(BF16), preferred_element_type=F32)


def _rms_proj(x, norm_g, w, layer):
    n, d = x.shape
    dn = w.shape[-1]
    tm = _tile(n, ROW_TM)
    return pl.pallas_call(
        _rms_proj_kernel,
        out_shape=jax.ShapeDtypeStruct((n, dn), F32),
        grid=(n // tm,),
        in_specs=[
            pl.BlockSpec((tm, d), lambda i: (i, 0)),
            pl.BlockSpec((None, 1, d), lambda i: (layer, 0, 0)),
            _resident((None, d, dn), lambda i: (layer, 0, 0)),
        ],
        out_specs=pl.BlockSpec((tm, dn), lambda i: (i, 0)),
        compiler_params=_params("parallel"),
        name="rms_proj",
    )(x, norm_g, w)


def _matmul_res_kernel(res_ref, x_ref, w_ref, o_ref):
    o_ref[...] = res_ref[...] + jnp.dot(x_ref[...].astype(BF16), w_ref[...], preferred_element_type=F32)


def _matmul_res(res, x, w, layer):
    n, d = res.shape
    dk = x.shape[-1]
    tm = _tile(n, ROW_TM)
    return pl.pallas_call(
        _matmul_res_kernel,
        out_shape=jax.ShapeDtypeStruct((n, d), F32),
        grid=(n // tm,),
        in_specs=[
            pl.BlockSpec((tm, d), lambda i: (i, 0)),
            pl.BlockSpec((tm, dk), lambda i: (i, 0)),
            _resident((None, dk, d), lambda i: (layer, 0, 0)),
        ],
        out_specs=pl.BlockSpec((tm, d), lambda i: (i, 0)),
        compiler_params=_params("parallel"),
        name="matmul_res",
    )(res, x, w)


def _head_norm_swish(y, g, b):
    yc = y - jnp.mean(y, axis=-1, keepdims=True)
    var = jnp.mean(yc * yc, axis=-1, keepdims=True)
    return _silu(yc * lax.rsqrt(var + EPS) * g + b)


def _mix_prompt_kernel(h_ref, mg_ref, wi_ref, pw_ref, ps_ref, cw_ref, cb_ref, ng_ref, nb_ref, wo_ref,
                       o_ref, pst_ref, cst_ref, pext, pool_out, *slabs, ts, conv_hist):
    t = pl.program_id(1)
    d_pool = pext.shape[-1]
    pool_group = d_pool // len(POOL_WINDOWS)
    heads = len(slabs) // 2
    gs, ys = slabs[:heads], slabs[heads:]
    hd = gs[0].shape[-1]
    d_conv = heads * hd
    conv_width = conv_hist + 1
    ph = pext.shape[0] - ts
    gh = gs[0].shape[0] - ts

    @pl.when(t == 0)
    def _():
        pext[0:ph, :] = jnp.zeros((ph, d_pool), F32)
        for c in range(heads):
            gs[c][0:gh, :] = jnp.zeros((gh, hd), F32)

    x = h_ref[...]
    xn = _rms(x, mg_ref[...]).astype(BF16)
    pext[ph:ph + ts, :] = jnp.dot(xn, wi_ref[:, 0:d_pool], preferred_element_type=F32)

    pos = t * ts + lax.broadcasted_iota(jnp.int32, (ts, 1), 0)
    for g, w in enumerate(POOL_WINDOWS):
        sl = slice(g * pool_group, (g + 1) * pool_group)
        cur = pext[ph:ph + ts, sl]
        acc = cur
        for k in range(1, w):
            acc = acc + pext[ph - k:ph - k + ts, sl]
        cnt = jnp.minimum(pos + 1, w).astype(F32)
        dlt = (acc / cnt - cur).astype(BF16)
        y = jnp.dot(dlt, pw_ref[g], preferred_element_type=F32) * ps_ref[:, sl]
        pool_out[:, sl] = y.astype(BF16)

    groups = CONV_CHUNK // (2 * SUBLANE)

    def conv_head(c):
        hs = slice(c * hd, (c + 1) * hd)
        bias = jnp.broadcast_to(cb_ref[:, hs], (SUBLANE, hd))
        for r0 in range(0, ts, CONV_CHUNK):
            even = [bias] * groups
            odd = [bias] * groups
            for j in range(conv_width + 1):
                rows = [gs[c][pl.ds(r0 + q * 2 * SUBLANE + (gh - conv_hist) + j, SUBLANE, stride=2), :]
                        for q in range(groups)]
                if j < conv_width:
                    wk = cw_ref[j, c]
                    even = [a + x * wk for a, x in zip(even, rows)]
                if j >= 1:
                    wk = cw_ref[j - 1, c]
                    odd = [a + x * wk for a, x in zip(odd, rows)]
            for q in range(groups):
                for phase, acc in enumerate((even[q], odd[q])):
                    y = _head_norm_swish(acc, ng_ref[:, hs], nb_ref[:, hs])
                    ys[c][pl.ds(r0 + q * 2 * SUBLANE + phase, SUBLANE, stride=2), :] = y

    def glu_pair(c0):
        lo = d_pool + c0 * hd
        a = jnp.dot(xn, wi_ref[:, lo:lo + 2 * hd], preferred_element_type=F32)
        b = jnp.dot(xn, wi_ref[:, lo + d_conv:lo + d_conv + 2 * hd], preferred_element_type=F32)
        return a * jax.nn.sigmoid(b)

    mixed = jnp.dot(pool_out[...], wo_ref[0:d_pool, :], preferred_element_type=F32)
    glu = glu_pair(0)
    for c0 in range(0, heads, 2):
        gs[c0][gh:gh + ts, :] = glu[:, 0:hd]
        gs[c0 + 1][gh:gh + ts, :] = glu[:, hd:2 * hd]
        if c0 + 2 < heads:
            glu = glu_pair(c0 + 2)
        conv_head(c0)
        conv_head(c0 + 1)
        pair = jnp.concatenate([ys[c0][...], ys[c0 + 1][...]], axis=1).astype(BF16)
        lo = d_pool + c0 * hd
        mixed = mixed + jnp.dot(pair, wo_ref[lo:lo + 2 * hd, :], preferred_element_type=F32)
    o_ref[...] = x + mixed

    pst_ref[...] = pext[ts:ts + ph, :]
    pext[0:ph, :] = pext[ts:ts + ph, :]
    for c in range(heads):
        cst_ref[:, c * hd:(c + 1) * hd] = gs[c][ts:ts + gh, :]
        gs[c][0:gh, :] = gs[c][ts:ts + gh, :]


def _mix_prompt(h, batch, seq, mix_norm, w_in, pool_w, pool_scale, conv_wb, conv_b, norm_g, norm_b, w_out, layer,
                pool_hist, conv_hist):
    n, d = h.shape
    d_pool = pool_scale.shape[-1]
    heads, hd = conv_wb.shape[2], conv_wb.shape[4]
    d_conv = heads * hd
    assert hd == LANE and w_in.shape[-1] == d_pool + 2 * d_conv
    ts = _tile(seq, MIX_TS)
    assert ts % CONV_CHUNK == 0
    nt = seq // ts
    ph = -(-pool_hist // SUBLANE) * SUBLANE
    gh = -(-conv_hist // SUBLANE) * SUBLANE
    row = lambda b, t: (b * nt + t, 0)
    vec = lambda b, t: (layer, 0, 0)
    out, pool_tail, conv_tail = pl.pallas_call(
        functools.partial(_mix_prompt_kernel, ts=ts, conv_hist=conv_hist),
        out_shape=(jax.ShapeDtypeStruct((n, d), F32), jax.ShapeDtypeStruct((batch, ph, d_pool), F32),
                   jax.ShapeDtypeStruct((batch, gh, d_conv), F32)),
        grid=(batch, nt),
        in_specs=[
            pl.BlockSpec((ts, d), row),
            pl.BlockSpec((None, 1, d), vec),
            _resident((None, d, d_pool + 2 * d_conv), vec),
            _resident((None,) + pool_w.shape[1:], lambda b, t: (layer, 0, 0, 0)),
            pl.BlockSpec((None, 1, d_pool), vec),
            _resident((None,) + conv_wb.shape[1:], lambda b, t: (layer, 0, 0, 0, 0)),
            pl.BlockSpec((None, 1, d_conv), vec),
            pl.BlockSpec((None, 1, d_conv), vec),
            pl.BlockSpec((None, 1, d_conv), vec),
            _resident((None, d_pool + d_conv, d), vec),
        ],
        out_specs=(pl.BlockSpec((ts, d), row),
                   pl.BlockSpec((None, ph, d_pool), lambda b, t: (b, 0, 0)),
                   pl.BlockSpec((None, gh, d_conv), lambda b, t: (b, 0, 0))),
        scratch_shapes=[
            pltpu.VMEM((ph + ts, d_pool), F32),
            pltpu.VMEM((ts, d_pool), BF16),
            *[pltpu.VMEM((gh + ts, hd), F32) for _ in range(heads)],
            *[pltpu.VMEM((ts, hd), F32) for _ in range(heads)],
        ],
        compiler_params=_params("arbitrary", "arbitrary", vmem_limit=MIX_VMEM_LIMIT),
        name="mix_prompt",
    )(h, mix_norm, w_in, pool_w, pool_scale, conv_wb, conv_b, norm_g, norm_b, w_out)
    return out, pool_tail[:, ph - pool_hist:], conv_tail[:, gh - conv_hist:]


def _mix_sample_kernel(pe_ref, ge_ref, pw_ref, ps_ref, cw_ref, cb_ref, ng_ref, nb_ref,
                       po_ref, co_ref, dbuf, *, steps, pool_hist, conv_hist):
    g = pl.program_id(0)
    bs, cb = pe_ref.shape[1], pe_ref.shape[2]
    head_dim = ng_ref.shape[-1] * len(POOL_WINDOWS) // CONV_HEADS
    conv_width = conv_hist + 1

    for t in range(steps):
        cur = pe_ref[pool_hist + t]
        acc = cur
        dlt = jnp.zeros_like(cur)
        k = 1
        for gi, w in enumerate(POOL_WINDOWS):
            while k < w:
                acc = acc + pe_ref[pool_hist + t - k]
                k += 1
            cnt = float(min(PAST_LEN + t + 1, w))
            dlt = jnp.where(g == gi, acc / cnt - cur, dlt)
        dbuf[t] = dlt.astype(BF16)

        y = jnp.broadcast_to(cb_ref[...], (bs, cb))
        for k in range(conv_width):
            y = y + ge_ref[t + k] * cw_ref[k:k + 1, :]
        for hh in range(cb // head_dim):
            hs = slice(hh * head_dim, (hh + 1) * head_dim)
            co_ref[t, :, hs] = _head_norm_swish(y[:, hs], ng_ref[:, hs], nb_ref[:, hs]).astype(BF16)

    dall = dbuf[...].reshape(steps * bs, cb)
    y = jnp.dot(dall, pw_ref[...], preferred_element_type=F32) * ps_ref[...]
    po_ref[...] = y.reshape(steps, bs, cb).astype(BF16)


def _mix_sample(pext_t, gext_t, pool_w, pool_scale, conv_w, conv_b, norm_g, norm_b, layer, steps,
                pool_hist, conv_hist):
    rp, bs, d_pool = pext_t.shape
    rg, _, d_conv = gext_t.shape
    ng = len(POOL_WINDOWS)
    cb = d_pool // ng
    assert d_conv // ng == cb and cb % (d_conv // CONV_HEADS) == 0
    vec = lambda g: (layer, 0, g)
    return pl.pallas_call(
        functools.partial(_mix_sample_kernel, steps=steps, pool_hist=pool_hist, conv_hist=conv_hist),
        out_shape=(jax.ShapeDtypeStruct((steps, bs, d_pool), BF16),
                   jax.ShapeDtypeStruct((steps, bs, d_conv), BF16)),
        grid=(ng,),
        in_specs=[
            pl.BlockSpec((rp, bs, cb), lambda g: (0, 0, g)),
            pl.BlockSpec((rg, bs, cb), lambda g: (0, 0, g)),
            pl.BlockSpec((None, None, cb, cb), lambda g: (layer, g, 0, 0)),
            pl.BlockSpec((None, 1, cb), vec),
            pl.BlockSpec((None, conv_hist + 1, cb), vec),
            pl.BlockSpec((None, 1, cb), vec),
            pl.BlockSpec((None, 1, cb), vec),
            pl.BlockSpec((None, 1, cb), vec),
        ],
        out_specs=(pl.BlockSpec((steps, bs, cb), lambda g: (0, 0, g)),
                   pl.BlockSpec((steps, bs, cb), lambda g: (0, 0, g))),
        scratch_shapes=[pltpu.VMEM((steps, bs, cb), BF16)],
        compiler_params=_params("parallel"),
        name="mix_sample",
    )(pext_t, gext_t, pool_w, pool_scale, conv_w, conv_b, norm_g, norm_b)


def _softmax(s):
    e = jnp.exp(s - jnp.max(s, axis=-1, keepdims=True))
    return e / jnp.sum(e, axis=-1, keepdims=True)


def _attn_prompt_kernel(h_ref, g_ref, wq_ref, k_ref, v_ref, wo_ref, o_ref, *, heads):
    x = h_ref[...]
    d = x.shape[-1]
    hd = d // heads
    q = jnp.dot(_rms(x, g_ref[...]).astype(BF16), wq_ref[...], preferred_element_type=F32)
    acc = x
    for hh in range(heads):
        hs = slice(hh * hd, (hh + 1) * hd)
        s = lax.dot_general(q[:, hs].astype(BF16), k_ref[:, hs].astype(BF16),
                            (((1,), (1,)), ((), ())), preferred_element_type=F32) * (hd ** -0.5)
        p = _softmax(s).astype(BF16)
        oh = jnp.dot(p, v_ref[:, hs].astype(BF16), preferred_element_type=F32)
        acc = acc + jnp.dot(oh.astype(BF16), wo_ref[hs, :], preferred_element_type=F32)
    o_ref[...] = acc


def _attn_prompt(h, batch, seq, norm_g, w_q, mk, mv, w_o, layer, heads):
    n, d = h.shape
    n_mem = mk.shape[1]
    tq = _tile(seq, ROW_TM)
    nt = seq // tq
    row = lambda b, t: (b * nt + t, 0)
    return pl.pallas_call(
        functools.partial(_attn_prompt_kernel, heads=heads),
        out_shape=jax.ShapeDtypeStruct((n, d), F32),
        grid=(batch, nt),
        in_specs=[
            pl.BlockSpec((tq, d), row),
            pl.BlockSpec((None, 1, d), lambda b, t: (layer, 0, 0)),
            _resident((None, d, d), lambda b, t: (layer, 0, 0)),
            pl.BlockSpec((None, n_mem, d), lambda b, t: (b, 0, 0)),
            pl.BlockSpec((None, n_mem, d), lambda b, t: (b, 0, 0)),
            _resident((None, d, d), lambda b, t: (layer, 0, 0)),
        ],
        out_specs=pl.BlockSpec((tq, d), row),
        compiler_params=_params("parallel", "parallel"),
        name="attn_prompt",
    )(h, norm_g, w_q, mk, mv, w_o)


def _attn_sample_kernel(q_ref, k_ref, v_ref, o_ref):
    bb, t, _ = q_ref.shape
    _, n_mem, heads, hd = k_ref.shape
    q = q_ref[...]
    qh = jnp.concatenate([q[:, :, h * hd:(h + 1) * hd] for h in range(heads)], axis=1).astype(BF16)
    k = k_ref[...].reshape(bb, n_mem * heads, hd).astype(BF16)
    v = v_ref[...].reshape(bb, n_mem * heads, hd).astype(BF16)
    s = jnp.einsum("bqd,bkd->bqk", qh, k, preferred_element_type=F32) * (hd ** -0.5)
    q_head = lax.broadcasted_iota(jnp.int32, s.shape, 1) // t
    k_head = lax.broadcasted_iota(jnp.int32, s.shape, 2) % heads
    p = _softmax(jnp.where(q_head == k_head, s, -jnp.inf)).astype(BF16)
    o = jnp.einsum("bqk,bkd->bqd", p, v, preferred_element_type=F32)
    for h in range(heads):
        o_ref[:, :, h * hd:(h + 1) * hd] = o[:, h * t:(h + 1) * t, :]


def _attn_sample(q, cache_k, cache_v, layer):
    bs, t, d = q.shape
    _, _, n_mem, heads, hd = cache_k.shape
    bb = _tile(bs, SAMPLE_ATTN_BB)
    kv = pl.BlockSpec((None, bb, n_mem, heads, hd), lambda i: (layer, i, 0, 0, 0))
    return pl.pallas_call(
        _attn_sample_kernel,
        out_shape=jax.ShapeDtypeStruct((bs, t, d), F32),
        grid=(bs // bb,),
        in_specs=[pl.BlockSpec((bb, t, d), lambda i: (i, 0, 0)), kv, kv],
        out_specs=pl.BlockSpec((bb, t, d), lambda i: (i, 0, 0)),
        compiler_params=_params("parallel"),
        name="attn_sample",
    )(q, cache_k, cache_v)


def kernel(x_prompt, x_sample, state_pool, state_conv, cache_mem_k, cache_mem_v, mem_prompt, ffn1_norm, ffn1_w_gate, ffn1_w_up, ffn1_w_down, mix_norm, w_in, pool_w, pool_scale, conv_w, conv_b, conv_norm_g, conv_norm_b, w_out, xattn_norm, mem_norm, w_q, w_mk, w_mv, w_o, ffn2_norm, ffn2_w_gate, ffn2_w_up, ffn2_w_down, final_norm):
    batch, seq, d = x_prompt.shape
    bs, steps, _ = x_sample.shape
    depth = state_pool.shape[0]
    pool_hist, d_pool = state_pool.shape[2:]
    conv_hist, d_conv = state_conv.shape[2:]
    n_mem, heads, head_dim = cache_mem_k.shape[2:]

    bf = lambda w: w.astype(BF16)
    ffn_w = ((ffn1_w_gate, ffn1_w_up, ffn1_w_down), (ffn2_w_gate, ffn2_w_up, ffn2_w_down))
    w_in_b, pool_w_b, w_out_b = bf(w_in), bf(pool_w), bf(w_out)
    w_q_b, w_o_b = bf(w_q), bf(w_o)

    vec = lambda v: v.reshape(v.shape[0], 1, v.shape[1])
    ffn_norms = (vec(ffn1_norm), vec(ffn2_norm))
    mix_norm_v, xattn_norm_v, mem_norm_v = vec(mix_norm), vec(xattn_norm), vec(mem_norm)
    pool_scale_v, conv_b_v, norm_g_v, norm_b_v = vec(pool_scale), vec(conv_b), vec(conv_norm_g), vec(conv_norm_b)
    final_g = final_norm.reshape(1, d)
    conv_head_dim = d_conv // CONV_HEADS
    conv_wb = jnp.broadcast_to(conv_w.reshape(depth, conv_hist + 1, CONV_HEADS, 1, conv_head_dim),
                               (depth, conv_hist + 1, CONV_HEADS, SUBLANE, conv_head_dim))

    mem = mem_prompt.reshape(batch * n_mem, d)

    hp = x_prompt.reshape(batch * seq, d)
    hs = x_sample.reshape(bs * steps, d)
    pool_p, conv_p, mk_p, mv_p, pool_s, conv_s = [], [], [], [], [], []

    for l in range(depth):
        last = l == depth - 1
        hs, *ffn_wb = _ffn(hs, ffn_norms[0], *ffn_w[0], l)
        hp = _ffn(hp, ffn_norms[0], *ffn_wb, l)

        hp, pool_tail, conv_tail = _mix_prompt(hp, batch, seq, mix_norm_v, w_in_b, pool_w_b, pool_scale_v, conv_wb,
                                               conv_b_v, norm_g_v, norm_b_v, w_out_b, l, pool_hist, conv_hist)
        pool_p.append(pool_tail)
        conv_p.append(conv_tail)

        p, glu = _proj_in(hs, mix_norm_v, w_in_b, l, d_pool, d_conv)
        p_ext = jnp.concatenate([state_pool[l], p.reshape(bs, steps, d_pool)], axis=1)
        g_ext = jnp.concatenate([state_conv[l], glu.reshape(bs, steps, d_conv)], axis=1)
        pool_s.append(p_ext[:, -pool_hist:])
        conv_s.append(g_ext[:, -conv_hist:])
        po, co = _mix_sample(p_ext.transpose(1, 0, 2), g_ext.transpose(1, 0, 2), pool_w_b, pool_scale_v, conv_w,
                             conv_b_v, norm_g_v, norm_b_v, l, steps, pool_hist, conv_hist)
        mixed = jnp.concatenate([po, co], axis=-1).transpose(1, 0, 2).reshape(bs * steps, d_pool + d_conv)
        hs = _matmul_res(hs, mixed, w_out_b, l)

        mk = _rms_proj(mem, mem_norm_v, w_mk, l)
        mv = _rms_proj(mem, mem_norm_v, w_mv, l)
        mk_p.append(mk.reshape(batch, n_mem, heads, head_dim))
        mv_p.append(mv.reshape(batch, n_mem, heads, head_dim))
        hp = _attn_prompt(hp, batch, seq, xattn_norm_v, w_q_b, mk.reshape(batch, n_mem, d),
                          mv.reshape(batch, n_mem, d), w_o_b, l, heads)
        q = _rms_proj(hs, xattn_norm_v, w_q_b, l)
        o = _attn_sample(q.reshape(bs, steps, d), cache_mem_k, cache_mem_v, l)
        hs = _matmul_res(hs, o.reshape(bs * steps, d), w_o_b, l)

        hs, *ffn_wb = _ffn(hs, ffn_norms[1], *ffn_w[1], l, final_g if last else None)
        hp = _ffn(hp, ffn_norms[1], *ffn_wb, l, final_g if last else None)

    return (hp.reshape(batch, seq, d), hs.reshape(bs, steps, d), jnp.stack(pool_p), jnp.stack(conv_p),
            jnp.stack(mk_p), jnp.stack(mv_p), jnp.stack(pool_s), jnp.stack(conv_s))
```

```python
import functools

import jax
import jax.numpy as jnp
from jax import lax
from jax.experimental import pallas as pl
from jax.experimental.pallas import tpu as pltpu

F32 = jnp.float32
BF16 = jnp.bfloat16

EPS = 1e-6
PAST_LEN = 16384
POOL_WINDOWS = (2, 4, 8, 16)
CONV_HEADS = 8

LANE = 128
SUBLANE = 8
VMEM_LIMIT = 56 * 1024 * 1024

FFN_TM = 1024
FFN_TF = 1024
FFN_SUB = 256
FFN_EMIT_TF = 256
FFN_VMEM_LIMIT = 60 * 1024 * 1024
MIX_VMEM_LIMIT = 60 * 1024 * 1024
ROW_TM = 512
MIX_TS = 512
CONV_CHUNK = 64
SAMPLE_ATTN_BB = 4


def _params(*sem, vmem_limit=VMEM_LIMIT):
    return pltpu.CompilerParams(dimension_semantics=sem, vmem_limit_bytes=vmem_limit)


def _resident(block_shape, index_map):
    return pl.BlockSpec(block_shape, index_map, pipeline_mode=pl.Buffered(1))


def _rms(x, g):
    return x * lax.rsqrt(jnp.mean(x * x, axis=-1, keepdims=True) + EPS) * g


def _silu(x):
    return x * jax.nn.sigmoid(x)


def _tile(n, t):
    t = min(n, t)
    assert n % t == 0, (n, t)
    return t


def _ffn_kernel(x_ref, g_ref, wg_ref, wu_ref, wd_ref, fg_ref, o_ref, *rest, final_norm, d_ff, sub, emit):
    if emit:
        wg_o, wu_o, wd_o, xn_ref = rest
    else:
        (xn_ref,) = rest
    j = pl.program_id(1)
    tf = wg_ref.shape[-1]
    nj = pl.cdiv(d_ff, tf)

    @pl.when(j == 0)
    def _():
        xn_ref[...] = _rms(x_ref[...], g_ref[...]).astype(BF16)
        o_ref[...] = jnp.zeros_like(o_ref)

    def columns(lo, n):
        if emit:
            wg_o[:, 0:n] = wg_ref[0, :, lo:lo + n].astype(BF16)
            wu_o[:, 0:n] = wu_ref[0, :, lo:lo + n].astype(BF16)
            wd_o[0:n, :] = wd_ref[0, lo:lo + n, :].astype(BF16)
            wg, wu, wd, lo = wg_o, wu_o, wd_o, 0
        else:
            wg, wu, wd = wg_ref, wu_ref, wd_ref
        xn = xn_ref[...]
        for c0 in range(lo, lo + n, sub):
            c1 = min(c0 + sub, lo + n)
            gate = jnp.dot(xn, wg[:, c0:c1], preferred_element_type=F32)
            up = jnp.dot(xn, wu[:, c0:c1], preferred_element_type=F32)
            hmid = (_silu(gate) * up).astype(BF16)
            o_ref[...] += jnp.dot(hmid, wd[c0:c1, :], preferred_element_type=F32)

    done = nj * tf - d_ff
    if done == 0:
        columns(0, tf)
    else:
        pl.when(j < nj - 1)(lambda: columns(0, tf))
        pl.when(j == nj - 1)(lambda: columns(done, tf - done))

    @pl.when(j == nj - 1)
    def _():
        y = x_ref[...] + 0.5 * o_ref[...]
        if final_norm:
            y = _rms(y, fg_ref[...])
        o_ref[...] = y


def _ffn(x, norm_g, wg, wu, wd, layer, final_g=None):
    n, d = x.shape
    d_ff = wg.shape[-1]
    emit = wg.ndim == 3
    tm = _tile(n, FFN_TM)
    tf, sub = (FFN_EMIT_TF, FFN_EMIT_TF) if emit else (FFN_TF, FFN_SUB)
    tf = min(tf, d_ff)
    assert tf % LANE == 0 and d_ff % LANE == 0
    final_norm = final_g is not None
    if final_g is None:
        final_g = jnp.ones((1, d), F32)
    start = lambda j: pl.multiple_of(jnp.minimum(j * tf, d_ff - tf), LANE)
    el = pl.Element
    bf16_specs = [
        pl.BlockSpec((el(d), el(tf)), lambda i, j: (0, start(j))),
        pl.BlockSpec((el(d), el(tf)), lambda i, j: (0, start(j))),
        pl.BlockSpec((el(tf), el(d)), lambda i, j: (start(j), 0)),
    ]
    f32_specs = [
        pl.BlockSpec((el(1), el(d), el(tf)), lambda i, j: (layer, 0, start(j))),
        pl.BlockSpec((el(1), el(d), el(tf)), lambda i, j: (layer, 0, start(j))),
        pl.BlockSpec((el(1), el(tf), el(d)), lambda i, j: (layer, start(j), 0)),
    ]
    out_shape = [jax.ShapeDtypeStruct((n, d), F32)]
    out_specs = [pl.BlockSpec((tm, d), lambda i, j: (i, 0))]
    if emit:
        assert n == tm, "the emitted casts are written by a single row tile"
        out_shape += [jax.ShapeDtypeStruct((d, d_ff), BF16), jax.ShapeDtypeStruct((d, d_ff), BF16),
                      jax.ShapeDtypeStruct((d_ff, d), BF16)]
        out_specs += [pl.BlockSpec((d, tf), lambda i, j: (0, j)), pl.BlockSpec((d, tf), lambda i, j: (0, j)),
                      pl.BlockSpec((tf, d), lambda i, j: (j, 0))]
    outs = pl.pallas_call(
        functools.partial(_ffn_kernel, final_norm=final_norm, d_ff=d_ff, sub=sub, emit=emit),
        out_shape=out_shape,
        grid=(n // tm, pl.cdiv(d_ff, tf)),
        in_specs=[
            _resident((tm, d), lambda i, j: (i, 0)),
            pl.BlockSpec((None, 1, d), lambda i, j: (layer, 0, 0)),
            *(f32_specs if emit else bf16_specs),
            pl.BlockSpec((1, d), lambda i, j: (0, 0)),
        ],
        out_specs=out_specs,
        scratch_shapes=[pltpu.VMEM((tm, d), BF16)],
        compiler_params=_params("parallel", "arbitrary", vmem_limit=FFN_VMEM_LIMIT),
        name="ffn_emit" if emit else "ffn",
    )(x, norm_g, wg, wu, wd, final_g)
    return outs if emit else outs[0]


def _proj_in_kernel(x_ref, g_ref, w_ref, p_ref, glu_ref, *, d_pool, d_conv):
    xn = _rms(x_ref[...], g_ref[...]).astype(BF16)
    z = jnp.dot(xn, w_ref[...], preferred_element_type=F32)
    p_ref[...] = z[:, :d_pool]
    glu_ref[...] = z[:, d_pool:d_pool + d_conv] * jax.nn.sigmoid(z[:, d_pool + d_conv:])


def _proj_in(x, norm_g, w_in, layer, d_pool, d_conv):
    n, d = x.shape
    tm = _tile(n, ROW_TM)
    dz = w_in.shape[-1]
    return pl.pallas_call(
        functools.partial(_proj_in_kernel, d_pool=d_pool, d_conv=d_conv),
        out_shape=(jax.ShapeDtypeStruct((n, d_pool), F32), jax.ShapeDtypeStruct((n, d_conv), F32)),
        grid=(n // tm,),
        in_specs=[
            pl.BlockSpec((tm, d), lambda i: (i, 0)),
            pl.BlockSpec((None, 1, d), lambda i: (layer, 0, 0)),
            _resident((None, d, dz), lambda i: (layer, 0, 0)),
        ],
        out_specs=(pl.BlockSpec((tm, d_pool), lambda i: (i, 0)),
                   pl.BlockSpec((tm, d_conv), lambda i: (i, 0))),
        compiler_params=_params("parallel"),
        name="proj_in",
    )(x, norm_g, w_in)


def _rms_proj_kernel(x_ref, g_ref, w_ref, o_ref):
    xn = _rms(x_ref[...], g_ref[...]).astype(BF16)
    o_ref[...] = jnp.dot(xn, w_ref[...].astype(BF16), preferred_element_type=F32)


def _rms_proj(x, norm_g, w, layer):
    n, d = x.shape
    dn = w.shape[-1]
    tm = _tile(n, ROW_TM)
    return pl.pallas_call(
        _rms_proj_kernel,
        out_shape=jax.ShapeDtypeStruct((n, dn), F32),
        grid=(n // tm,),
        in_specs=[
            pl.BlockSpec((tm, d), lambda i: (i, 0)),
            pl.BlockSpec((None, 1, d), lambda i: (layer, 0, 0)),
            _resident((None, d, dn), lambda i: (layer, 0, 0)),
        ],
        out_specs=pl.BlockSpec((tm, dn), lambda i: (i, 0)),
        compiler_params=_params("parallel"),
        name="rms_proj",
    )(x, norm_g, w)


def _matmul_res_kernel(res_ref, *refs):
    *x_refs, w_ref, o_ref = refs
    acc = res_ref[...]
    k0 = 0
    for x_ref in x_refs:
        k1 = k0 + x_ref.shape[-1]
        acc = acc + jnp.dot(x_ref[...].astype(BF16), w_ref[k0:k1, :], preferred_element_type=F32)
        k0 = k1
    o_ref[...] = acc


def _matmul_res(res, xs, w, layer):
    n, d = res.shape
    dk = sum(x.shape[-1] for x in xs)
    assert dk == w.shape[1]
    tm = _tile(n, ROW_TM)
    return pl.pallas_call(
        _matmul_res_kernel,
        out_shape=jax.ShapeDtypeStruct((n, d), F32),
        grid=(n // tm,),
        in_specs=[
            pl.BlockSpec((tm, d), lambda i: (i, 0)),
            *[pl.BlockSpec((tm, x.shape[-1]), lambda i: (i, 0)) for x in xs],
            _resident((None, dk, d), lambda i: (layer, 0, 0)),
        ],
        out_specs=pl.BlockSpec((tm, d), lambda i: (i, 0)),
        compiler_params=_params("parallel"),
        name="matmul_res",
    )(res, *xs, w)


def _head_norm_swish(y, g, b):
    yc = y - jnp.mean(y, axis=-1, keepdims=True)
    var = jnp.mean(yc * yc, axis=-1, keepdims=True)
    return _silu(yc * lax.rsqrt(var + EPS) * g + b)


def _mix_prompt_kernel(h_ref, mg_ref, wi_ref, pw_ref, ps_ref, cw_ref, cb_ref, ng_ref, nb_ref, wo_ref,
                       o_ref, pst_ref, cst_ref, pext, pool_out, *slabs, ts, conv_hist):
    t = pl.program_id(1)
    d_pool = pext.shape[-1]
    pool_group = d_pool // len(POOL_WINDOWS)
    heads = len(slabs) // 2
    gs, ys = slabs[:heads], slabs[heads:]
    hd = gs[0].shape[-1]
    d_conv = heads * hd
    conv_width = conv_hist + 1
    ph = pext.shape[0] - ts
    gh = gs[0].shape[0] - ts

    @pl.when(t == 0)
    def _():
        pext[0:ph, :] = jnp.zeros((ph, d_pool), F32)
        for c in range(heads):
            gs[c][0:gh, :] = jnp.zeros((gh, hd), F32)

    x = h_ref[...]
    xn = _rms(x, mg_ref[...]).astype(BF16)
    pext[ph:ph + ts, :] = jnp.dot(xn, wi_ref[:, 0:d_pool], preferred_element_type=F32)

    pos = t * ts + lax.broadcasted_iota(jnp.int32, (ts, 1), 0)
    for g, w in enumerate(POOL_WINDOWS):
        sl = slice(g * pool_group, (g + 1) * pool_group)
        cur = pext[ph:ph + ts, sl]
        acc = cur
        for k in range(1, w):
            acc = acc + pext[ph - k:ph - k + ts, sl]
        cnt = jnp.minimum(pos + 1, w).astype(F32)
        dlt = (acc / cnt - cur).astype(BF16)
        y = jnp.dot(dlt, pw_ref[g], preferred_element_type=F32) * ps_ref[:, sl]
        pool_out[:, sl] = y.astype(BF16)

    groups = CONV_CHUNK // (2 * SUBLANE)

    def conv_head(c):
        hs = slice(c * hd, (c + 1) * hd)
        bias = jnp.broadcast_to(cb_ref[:, hs], (SUBLANE, hd))
        for r0 in range(0, ts, CONV_CHUNK):
            even = [bias] * groups
            odd = [bias] * groups
            for j in range(conv_width + 1):
                rows = [gs[c][pl.ds(r0 + q * 2 * SUBLANE + (gh - conv_hist) + j, SUBLANE, stride=2), :]
                        for q in range(groups)]
                if j < conv_width:
                    wk = cw_ref[j, c]
                    even = [a + x * wk for a, x in zip(even, rows)]
                if j >= 1:
                    wk = cw_ref[j - 1, c]
                    odd = [a + x * wk for a, x in zip(odd, rows)]
            for q in range(groups):
                for phase, acc in enumerate((even[q], odd[q])):
                    y = _head_norm_swish(acc, ng_ref[:, hs], nb_ref[:, hs])
                    ys[c][pl.ds(r0 + q * 2 * SUBLANE + phase, SUBLANE, stride=2), :] = y

    def glu_pair(c0):
        lo = d_pool + c0 * hd
        a = jnp.dot(xn, wi_ref[:, lo:lo + 2 * hd], preferred_element_type=F32)
        b = jnp.dot(xn, wi_ref[:, lo + d_conv:lo + d_conv + 2 * hd], preferred_element_type=F32)
        return a * jax.nn.sigmoid(b)

    mixed = jnp.dot(pool_out[...], wo_ref[0:d_pool, :], preferred_element_type=F32)
    glu = glu_pair(0)
    for c0 in range(0, heads, 2):
        gs[c0][gh:gh + ts, :] = glu[:, 0:hd]
        gs[c0 + 1][gh:gh + ts, :] = glu[:, hd:2 * hd]
        if c0 + 2 < heads:
            glu = glu_pair(c0 + 2)
        conv_head(c0)
        conv_head(c0 + 1)
        pair = jnp.concatenate([ys[c0][...], ys[c0 + 1][...]], axis=1).astype(BF16)
        lo = d_pool + c0 * hd
        mixed = mixed + jnp.dot(pair, wo_ref[lo:lo + 2 * hd, :], preferred_element_type=F32)
    o_ref[...] = x + mixed

    pst_ref[...] = pext[ts:ts + ph, :]
    pext[0:ph, :] = pext[ts:ts + ph, :]
    for c in range(heads):
        cst_ref[:, c * hd:(c + 1) * hd] = gs[c][ts:ts + gh, :]
        gs[c][0:gh, :] = gs[c][ts:ts + gh, :]


def _mix_prompt(h, batch, seq, mix_norm, w_in, pool_w, pool_scale, conv_wb, conv_b, norm_g, norm_b, w_out, layer,
                pool_hist, conv_hist):
    n, d = h.shape
    d_pool = pool_scale.shape[-1]
    heads, hd = conv_wb.shape[2], conv_wb.shape[4]
    d_conv = heads * hd
    assert hd == LANE and w_in.shape[-1] == d_pool + 2 * d_conv
    ts = _tile(seq, MIX_TS)
    assert ts % CONV_CHUNK == 0
    nt = seq // ts
    ph = -(-pool_hist // SUBLANE) * SUBLANE
    gh = -(-conv_hist // SUBLANE) * SUBLANE
    row = lambda b, t: (b * nt + t, 0)
    vec = lambda b, t: (layer, 0, 0)
    out, pool_tail, conv_tail = pl.pallas_call(
        functools.partial(_mix_prompt_kernel, ts=ts, conv_hist=conv_hist),
        out_shape=(jax.ShapeDtypeStruct((n, d), F32), jax.ShapeDtypeStruct((batch, ph, d_pool), F32),
                   jax.ShapeDtypeStruct((batch, gh, d_conv), F32)),
        grid=(batch, nt),
        in_specs=[
            pl.BlockSpec((ts, d), row),
            pl.BlockSpec((None, 1, d), vec),
            _resident((None, d, d_pool + 2 * d_conv), vec),
            _resident((None,) + pool_w.shape[1:], lambda b, t: (layer, 0, 0, 0)),
            pl.BlockSpec((None, 1, d_pool), vec),
            _resident((None,) + conv_wb.shape[1:], lambda b, t: (layer, 0, 0, 0, 0)),
            pl.BlockSpec((None, 1, d_conv), vec),
            pl.BlockSpec((None, 1, d_conv), vec),
            pl.BlockSpec((None, 1, d_conv), vec),
            _resident((None, d_pool + d_conv, d), vec),
        ],
        out_specs=(pl.BlockSpec((ts, d), row),
                   pl.BlockSpec((None, ph, d_pool), lambda b, t: (b, 0, 0)),
                   pl.BlockSpec((None, gh, d_conv), lambda b, t: (b, 0, 0))),
        scratch_shapes=[
            pltpu.VMEM((ph + ts, d_pool), F32),
            pltpu.VMEM((ts, d_pool), BF16),
            *[pltpu.VMEM((gh + ts, hd), F32) for _ in range(heads)],
            *[pltpu.VMEM((ts, hd), F32) for _ in range(heads)],
        ],
        compiler_params=_params("arbitrary", "arbitrary", vmem_limit=MIX_VMEM_LIMIT),
        name="mix_prompt",
    )(h, mix_norm, w_in, pool_w, pool_scale, conv_wb, conv_b, norm_g, norm_b, w_out)
    return out, pool_tail[:, ph - pool_hist:], conv_tail[:, gh - conv_hist:]


def _mix_sample_kernel(sp_ref, p_ref, sc_ref, glu_ref, pw_ref, ps_ref, cw_ref, cb_ref, ng_ref, nb_ref,
                       po_ref, co_ref, pn_ref, cn_ref, pe, ge, dbuf, *, steps, pool_hist, conv_hist):
    g = pl.program_id(0)
    _, bs, cb = pe.shape
    head_dim = ng_ref.shape[-1] * len(POOL_WINDOWS) // CONV_HEADS
    conv_width = conv_hist + 1

    for r in range(pool_hist):
        pe[r] = sp_ref[:, r, :]
    for r in range(conv_hist):
        ge[r] = sc_ref[:, r, :]
    for t in range(steps):
        pe[pool_hist + t] = p_ref[:, t, :]
        ge[conv_hist + t] = glu_ref[:, t, :]

    for t in range(steps):
        cur = pe[pool_hist + t]
        acc = cur
        dlt = jnp.zeros_like(cur)
        k = 1
        for gi, w in enumerate(POOL_WINDOWS):
            while k < w:
                acc = acc + pe[pool_hist + t - k]
                k += 1
            cnt = float(min(PAST_LEN + t + 1, w))
            dlt = jnp.where(g == gi, acc / cnt - cur, dlt)
        dbuf[t] = dlt.astype(BF16)

        y = jnp.broadcast_to(cb_ref[...], (bs, cb))
        for k in range(conv_width):
            y = y + ge[t + k] * cw_ref[k:k + 1, :]
        for hh in range(cb // head_dim):
            hs = slice(hh * head_dim, (hh + 1) * head_dim)
            co_ref[:, t, hs] = _head_norm_swish(y[:, hs], ng_ref[:, hs], nb_ref[:, hs])

    dall = dbuf[...].reshape(steps * bs, cb)
    y = jnp.dot(dall, pw_ref[...], preferred_element_type=F32) * ps_ref[...]
    for t in range(steps):
        po_ref[:, t, :] = y[t * bs:(t + 1) * bs, :]

    for i in range(pool_hist):
        pn_ref[:, i, :] = pe[i + steps]
    for i in range(conv_hist):
        cn_ref[:, i, :] = ge[i + steps]


def _mix_sample(state_pool, p, state_conv, glu, pool_w, pool_scale, conv_w, conv_b, norm_g, norm_b, layer):
    _, bs, pool_hist, d_pool = state_pool.shape
    _, _, conv_hist, d_conv = state_conv.shape
    steps = p.shape[1]
    ng = len(POOL_WINDOWS)
    cb = d_pool // ng
    assert d_conv // ng == cb and cb % (d_conv // CONV_HEADS) == 0
    vec = lambda g: (layer, 0, g)
    seq = lambda rows: pl.BlockSpec((bs, rows, cb), lambda g: (0, 0, g))
    return pl.pallas_call(
        functools.partial(_mix_sample_kernel, steps=steps, pool_hist=pool_hist, conv_hist=conv_hist),
        out_shape=(jax.ShapeDtypeStruct((bs, steps, d_pool), F32), jax.ShapeDtypeStruct((bs, steps, d_conv), F32),
                   jax.ShapeDtypeStruct((bs, pool_hist, d_pool), F32),
                   jax.ShapeDtypeStruct((bs, conv_hist, d_conv), F32)),
        grid=(ng,),
        in_specs=[
            pl.BlockSpec((None, bs, pool_hist, cb), lambda g: (layer, 0, 0, g)),
            seq(steps),
            pl.BlockSpec((None, bs, conv_hist, cb), lambda g: (layer, 0, 0, g)),
            seq(steps),
            pl.BlockSpec((None, None, cb, cb), lambda g: (layer, g, 0, 0)),
            pl.BlockSpec((None, 1, cb), vec),
            pl.BlockSpec((None, conv_hist + 1, cb), vec),
            pl.BlockSpec((None, 1, cb), vec),
            pl.BlockSpec((None, 1, cb), vec),
            pl.BlockSpec((None, 1, cb), vec),
        ],
        out_specs=(seq(steps), seq(steps), seq(pool_hist), seq(conv_hist)),
        scratch_shapes=[pltpu.VMEM((pool_hist + steps, bs, cb), F32), pltpu.VMEM((conv_hist + steps, bs, cb), F32),
                        pltpu.VMEM((steps, bs, cb), BF16)],
        compiler_params=_params("parallel"),
        name="mix_sample",
    )(state_pool, p, state_conv, glu, pool_w, pool_scale, conv_w, conv_b, norm_g, norm_b)


def _softmax(s):
    e = jnp.exp(s - jnp.max(s, axis=-1, keepdims=True))
    return e / jnp.sum(e, axis=-1, keepdims=True)


def _attn_prompt_kernel(h_ref, g_ref, wq_ref, k_ref, v_ref, wo_ref, o_ref, *, heads):
    x = h_ref[...]
    d = x.shape[-1]
    hd = d // heads
    q = jnp.dot(_rms(x, g_ref[...]).astype(BF16), wq_ref[...], preferred_element_type=F32)
    acc = x
    for hh in range(heads):
        hs = slice(hh * hd, (hh + 1) * hd)
        s = lax.dot_general(q[:, hs].astype(BF16), k_ref[:, hs].astype(BF16),
                            (((1,), (1,)), ((), ())), preferred_element_type=F32) * (hd ** -0.5)
        p = _softmax(s).astype(BF16)
        oh = jnp.dot(p, v_ref[:, hs].astype(BF16), preferred_element_type=F32)
        acc = acc + jnp.dot(oh.astype(BF16), wo_ref[hs, :], preferred_element_type=F32)
    o_ref[...] = acc


def _attn_prompt(h, batch, seq, norm_g, w_q, mk, mv, w_o, layer, heads):
    n, d = h.shape
    n_mem = mk.shape[1]
    tq = _tile(seq, ROW_TM)
    nt = seq // tq
    row = lambda b, t: (b * nt + t, 0)
    return pl.pallas_call(
        functools.partial(_attn_prompt_kernel, heads=heads),
        out_shape=jax.ShapeDtypeStruct((n, d), F32),
        grid=(batch, nt),
        in_specs=[
            pl.BlockSpec((tq, d), row),
            pl.BlockSpec((None, 1, d), lambda b, t: (layer, 0, 0)),
            _resident((None, d, d), lambda b, t: (layer, 0, 0)),
            pl.BlockSpec((None, n_mem, d), lambda b, t: (b, 0, 0)),
            pl.BlockSpec((None, n_mem, d), lambda b, t: (b, 0, 0)),
            _resident((None, d, d), lambda b, t: (layer, 0, 0)),
        ],
        out_specs=pl.BlockSpec((tq, d), row),
        compiler_params=_params("parallel", "parallel"),
        name="attn_prompt",
    )(h, norm_g, w_q, mk, mv, w_o)


def _attn_sample_kernel(q_ref, k_ref, v_ref, o_ref):
    bb, t, _ = q_ref.shape
    _, n_mem, heads, hd = k_ref.shape
    q = q_ref[...]
    qh = jnp.concatenate([q[:, :, h * hd:(h + 1) * hd] for h in range(heads)], axis=1).astype(BF16)
    k = k_ref[...].reshape(bb, n_mem * heads, hd).astype(BF16)
    v = v_ref[...].reshape(bb, n_mem * heads, hd).astype(BF16)
    s = jnp.einsum("bqd,bkd->bqk", qh, k, preferred_element_type=F32) * (hd ** -0.5)
    q_head = lax.broadcasted_iota(jnp.int32, s.shape, 1) // t
    k_head = lax.broadcasted_iota(jnp.int32, s.shape, 2) % heads
    p = _softmax(jnp.where(q_head == k_head, s, -jnp.inf)).astype(BF16)
    o = jnp.einsum("bqk,bkd->bqd", p, v, preferred_element_type=F32)
    for h in range(heads):
        o_ref[:, :, h * hd:(h + 1) * hd] = o[:, h * t:(h + 1) * t, :]


def _attn_sample(q, cache_k, cache_v, layer):
    bs, t, d = q.shape
    _, _, n_mem, heads, hd = cache_k.shape
    bb = _tile(bs, SAMPLE_ATTN_BB)
    kv = pl.BlockSpec((None, bb, n_mem, heads, hd), lambda i: (layer, i, 0, 0, 0))
    return pl.pallas_call(
        _attn_sample_kernel,
        out_shape=jax.ShapeDtypeStruct((bs, t, d), F32),
        grid=(bs // bb,),
        in_specs=[pl.BlockSpec((bb, t, d), lambda i: (i, 0, 0)), kv, kv],
        out_specs=pl.BlockSpec((bb, t, d), lambda i: (i, 0, 0)),
        compiler_params=_params("parallel"),
        name="attn_sample",
    )(q, cache_k, cache_v)


def kernel(x_prompt, x_sample, state_pool, state_conv, cache_mem_k, cache_mem_v, mem_prompt, ffn1_norm, ffn1_w_gate, ffn1_w_up, ffn1_w_down, mix_norm, w_in, pool_w, pool_scale, conv_w, conv_b, conv_norm_g, conv_norm_b, w_out, xattn_norm, mem_norm, w_q, w_mk, w_mv, w_o, ffn2_norm, ffn2_w_gate, ffn2_w_up, ffn2_w_down, final_norm):
    batch, seq, d = x_prompt.shape
    bs, steps, _ = x_sample.shape
    depth = state_pool.shape[0]
    pool_hist, d_pool = state_pool.shape[2:]
    conv_hist, d_conv = state_conv.shape[2:]
    n_mem, heads, head_dim = cache_mem_k.shape[2:]

    bf = lambda w: w.astype(BF16)
    ffn_w = ((ffn1_w_gate, ffn1_w_up, ffn1_w_down), (ffn2_w_gate, ffn2_w_up, ffn2_w_down))
    w_in_b, pool_w_b, w_out_b = bf(w_in), bf(pool_w), bf(w_out)
    w_q_b, w_o_b = bf(w_q), bf(w_o)

    vec = lambda v: v.reshape(v.shape[0], 1, v.shape[1])
    ffn_norms = (vec(ffn1_norm), vec(ffn2_norm))
    mix_norm_v, xattn_norm_v, mem_norm_v = vec(mix_norm), vec(xattn_norm), vec(mem_norm)
    pool_scale_v, conv_b_v, norm_g_v, norm_b_v = vec(pool_scale), vec(conv_b), vec(conv_norm_g), vec(conv_norm_b)
    final_g = final_norm.reshape(1, d)
    conv_head_dim = d_conv // CONV_HEADS
    conv_wb = jnp.broadcast_to(conv_w.reshape(depth, conv_hist + 1, CONV_HEADS, 1, conv_head_dim),
                               (depth, conv_hist + 1, CONV_HEADS, SUBLANE, conv_head_dim))

    mem = mem_prompt.reshape(batch * n_mem, d)

    hp = x_prompt.reshape(batch * seq, d)
    hs = x_sample.reshape(bs * steps, d)
    pool_p, conv_p, mk_p, mv_p, pool_s, conv_s = [], [], [], [], [], []

    for l in range(depth):
        last = l == depth - 1
        hs, *ffn_wb = _ffn(hs, ffn_norms[0], *ffn_w[0], l)
        hp = _ffn(hp, ffn_norms[0], *ffn_wb, l)

        hp, pool_tail, conv_tail = _mix_prompt(hp, batch, seq, mix_norm_v, w_in_b, pool_w_b, pool_scale_v, conv_wb,
                                               conv_b_v, norm_g_v, norm_b_v, w_out_b, l, pool_hist, conv_hist)
        pool_p.append(pool_tail)
        conv_p.append(conv_tail)

        p, glu = _proj_in(hs, mix_norm_v, w_in_b, l, d_pool, d_conv)
        po, co, pool_new, conv_new = _mix_sample(state_pool, p.reshape(bs, steps, d_pool), state_conv,
                                                 glu.reshape(bs, steps, d_conv), pool_w_b, pool_scale_v, conv_w,
                                                 conv_b_v, norm_g_v, norm_b_v, l)
        pool_s.append(pool_new)
        conv_s.append(conv_new)
        hs = _matmul_res(hs, [po.reshape(bs * steps, d_pool), co.reshape(bs * steps, d_conv)], w_out_b, l)

        mk = _rms_proj(mem, mem_norm_v, w_mk, l)
        mv = _rms_proj(mem, mem_norm_v, w_mv, l)
        mk_p.append(mk.reshape(batch, n_mem, heads, head_dim))
        mv_p.append(mv.reshape(batch, n_mem, heads, head_dim))
        hp = _attn_prompt(hp, batch, seq, xattn_norm_v, w_q_b, mk.reshape(batch, n_mem, d),
                          mv.reshape(batch, n_mem, d), w_o_b, l, heads)
        q = _rms_proj(hs, xattn_norm_v, w_q_b, l)
        o = _attn_sample(q.reshape(bs, steps, d), cache_mem_k, cache_mem_v, l)
        hs = _matmul_res(hs, [o.reshape(bs * steps, d)], w_o_b, l)

        hs, *ffn_wb = _ffn(hs, ffn_norms[1], *ffn_w[1], l, final_g if last else None)
        hp = _ffn(hp, ffn_norms[1], *ffn_wb, l, final_g if last else None)

    return (hp.reshape(batch, seq, d), hs.reshape(bs, steps, d), jnp.stack(pool_p), jnp.stack(conv_p),
            jnp.stack(mk_p), jnp.stack(mv_p), jnp.stack(pool_s), jnp.stack(conv_s))
```

```python
import functools

import jax
import jax.numpy as jnp
from jax import lax
from jax.experimental import pallas as pl
from jax.experimental.pallas import tpu as pltpu

F32 = jnp.float32
BF16 = jnp.bfloat16

EPS = 1e-6
PAST_LEN = 16384
POOL_WINDOWS = (2, 4, 8, 16)
CONV_HEADS = 8

LANE = 128
SUBLANE = 8
VMEM_LIMIT = 56 * 1024 * 1024

FFN_TM = 1024
FFN_TF = 1024
FFN_SUB = 256
FFN_EMIT_TF = 256
FFN_VMEM_LIMIT = 60 * 1024 * 1024
MIX_VMEM_LIMIT = 60 * 1024 * 1024
ROW_TM = 512
EMIT_TM = 256
MIX_TS = 512
CONV_CHUNK = 32
SAMPLE_ATTN_BB = 4


def _params(*sem, vmem_limit=VMEM_LIMIT):
    return pltpu.CompilerParams(dimension_semantics=sem, vmem_limit_bytes=vmem_limit)


def _resident(block_shape, index_map):
    return pl.BlockSpec(block_shape, index_map, pipeline_mode=pl.Buffered(1))


def _rms(x, g):
    return x * lax.rsqrt(jnp.mean(x * x, axis=-1, keepdims=True) + EPS) * g


def _silu(x):
    return x * jax.nn.sigmoid(x)


def _tile(n, t):
    t = min(n, t)
    assert n % t == 0, (n, t)
    return t


def _ffn_kernel(x_ref, g_ref, wg_ref, wu_ref, wd_ref, fg_ref, o_ref, *rest, final_norm, d_ff, sub, emit):
    if emit:
        wg_o, wu_o, wd_o, xn_ref = rest
    else:
        (xn_ref,) = rest
    j = pl.program_id(1)
    tf = wg_ref.shape[-1]
    nj = pl.cdiv(d_ff, tf)

    @pl.when(j == 0)
    def _():
        xn_ref[...] = _rms(x_ref[...], g_ref[...]).astype(BF16)
        o_ref[...] = jnp.zeros_like(o_ref)

    def columns(lo, n):
        if emit:
            wg_o[:, 0:n] = wg_ref[0, :, lo:lo + n].astype(BF16)
            wu_o[:, 0:n] = wu_ref[0, :, lo:lo + n].astype(BF16)
            wd_o[0:n, :] = wd_ref[0, lo:lo + n, :].astype(BF16)
            wg, wu, wd, lo = wg_o, wu_o, wd_o, 0
        else:
            wg, wu, wd = wg_ref, wu_ref, wd_ref
        xn = xn_ref[...]
        for c0 in range(lo, lo + n, sub):
            c1 = min(c0 + sub, lo + n)
            gate = jnp.dot(xn, wg[:, c0:c1], preferred_element_type=F32)
            up = jnp.dot(xn, wu[:, c0:c1], preferred_element_type=F32)
            hmid = (_silu(gate) * up).astype(BF16)
            o_ref[...] += jnp.dot(hmid, wd[c0:c1, :], preferred_element_type=F32)

    done = nj * tf - d_ff
    if done == 0:
        columns(0, tf)
    else:
        pl.when(j < nj - 1)(lambda: columns(0, tf))
        pl.when(j == nj - 1)(lambda: columns(done, tf - done))

    @pl.when(j == nj - 1)
    def _():
        y = x_ref[...] + 0.5 * o_ref[...]
        if final_norm:
            y = _rms(y, fg_ref[...])
        o_ref[...] = y


def _ffn(x, norm_g, wg, wu, wd, layer, final_g=None):
    n, d = x.shape
    d_ff = wg.shape[-1]
    emit = wg.ndim == 3
    tm = _tile(n, FFN_TM)
    tf, sub = (FFN_EMIT_TF, FFN_EMIT_TF) if emit else (FFN_TF, FFN_SUB)
    tf = min(tf, d_ff)
    assert tf % LANE == 0 and d_ff % LANE == 0
    final_norm = final_g is not None
    if final_g is None:
        final_g = jnp.ones((1, d), F32)
    start = lambda j: pl.multiple_of(jnp.minimum(j * tf, d_ff - tf), LANE)
    el = pl.Element
    bf16_specs = [
        pl.BlockSpec((el(d), el(tf)), lambda i, j: (0, start(j))),
        pl.BlockSpec((el(d), el(tf)), lambda i, j: (0, start(j))),
        pl.BlockSpec((el(tf), el(d)), lambda i, j: (start(j), 0)),
    ]
    f32_specs = [
        pl.BlockSpec((el(1), el(d), el(tf)), lambda i, j: (layer, 0, start(j))),
        pl.BlockSpec((el(1), el(d), el(tf)), lambda i, j: (layer, 0, start(j))),
        pl.BlockSpec((el(1), el(tf), el(d)), lambda i, j: (layer, start(j), 0)),
    ]
    out_shape = [jax.ShapeDtypeStruct((n, d), F32)]
    out_specs = [pl.BlockSpec((tm, d), lambda i, j: (i, 0))]
    if emit:
        assert n == tm, "the emitted casts are written by a single row tile"
        out_shape += [jax.ShapeDtypeStruct((d, d_ff), BF16), jax.ShapeDtypeStruct((d, d_ff), BF16),
                      jax.ShapeDtypeStruct((d_ff, d), BF16)]
        out_specs += [pl.BlockSpec((d, tf), lambda i, j: (0, j)), pl.BlockSpec((d, tf), lambda i, j: (0, j)),
                      pl.BlockSpec((tf, d), lambda i, j: (j, 0))]
    outs = pl.pallas_call(
        functools.partial(_ffn_kernel, final_norm=final_norm, d_ff=d_ff, sub=sub, emit=emit),
        out_shape=out_shape,
        grid=(n // tm, pl.cdiv(d_ff, tf)),
        in_specs=[
            _resident((tm, d), lambda i, j: (i, 0)),
            pl.BlockSpec((None, 1, d), lambda i, j: (layer, 0, 0)),
            *(f32_specs if emit else bf16_specs),
            pl.BlockSpec((1, d), lambda i, j: (0, 0)),
        ],
        out_specs=out_specs,
        scratch_shapes=[pltpu.VMEM((tm, d), BF16)],
        compiler_params=_params("parallel", "arbitrary", vmem_limit=FFN_VMEM_LIMIT),
        name="ffn_emit" if emit else "ffn",
    )(x, norm_g, wg, wu, wd, final_g)
    return outs if emit else outs[0]


def _proj_in_kernel(x_ref, g_ref, w_ref, p_ref, glu_ref, *, d_pool, d_conv):
    xn = _rms(x_ref[...], g_ref[...]).astype(BF16)
    z = jnp.dot(xn, w_ref[...], preferred_element_type=F32)
    p_ref[...] = z[:, :d_pool]
    glu_ref[...] = z[:, d_pool:d_pool + d_conv] * jax.nn.sigmoid(z[:, d_pool + d_conv:])


def _proj_in(x, norm_g, w_in, layer, d_pool, d_conv):
    n, d = x.shape
    tm = _tile(n, ROW_TM)
    dz = w_in.shape[-1]
    return pl.pallas_call(
        functools.partial(_proj_in_kernel, d_pool=d_pool, d_conv=d_conv),
        out_shape=(jax.ShapeDtypeStruct((n, d_pool), F32), jax.ShapeDtypeStruct((n, d_conv), F32)),
        grid=(n // tm,),
        in_specs=[
            pl.BlockSpec((tm, d), lambda i: (i, 0)),
            pl.BlockSpec((None, 1, d), lambda i: (layer, 0, 0)),
            _resident((None, d, dz), lambda i: (layer, 0, 0)),
        ],
        out_specs=(pl.BlockSpec((tm, d_pool), lambda i: (i, 0)),
                   pl.BlockSpec((tm, d_conv), lambda i: (i, 0))),
        compiler_params=_params("parallel"),
        name="proj_in",
    )(x, norm_g, w_in)


def _bf16_weight(w_ref, wb_ref):
    if not wb_ref:
        return w_ref[...].astype(BF16)
    (wb_ref,) = wb_ref

    @pl.when(pl.program_id(0) == 0)
    def _():
        wb_ref[...] = w_ref[...].astype(BF16)

    return wb_ref[...]


def _emit_weight_out(w):
    return (jax.ShapeDtypeStruct((1,) + w.shape[1:], BF16),
            pl.BlockSpec((None,) + w.shape[1:], lambda i: (0, 0, 0)))


def _rms_proj_kernel(x_ref, g_ref, w_ref, o_ref, *wb_ref):
    xn = _rms(x_ref[...], g_ref[...]).astype(BF16)
    o_ref[...] = jnp.dot(xn, _bf16_weight(w_ref, wb_ref), preferred_element_type=F32)


def _rms_proj(x, norm_g, w, layer, emit=False):
    n, d = x.shape
    dn = w.shape[-1]
    tm = _tile(n, EMIT_TM if emit else ROW_TM)
    out_shape = [jax.ShapeDtypeStruct((n, dn), F32)]
    out_specs = [pl.BlockSpec((tm, dn), lambda i: (i, 0))]
    if emit:
        shape, spec = _emit_weight_out(w)
        out_shape.append(shape)
        out_specs.append(spec)
    outs = pl.pallas_call(
        _rms_proj_kernel,
        out_shape=out_shape,
        grid=(n // tm,),
        in_specs=[
            pl.BlockSpec((tm, d), lambda i: (i, 0)),
            pl.BlockSpec((None, 1, d), lambda i: (layer, 0, 0)),
            _resident((None, d, dn), lambda i: (layer, 0, 0)),
        ],
        out_specs=out_specs,
        compiler_params=_params("arbitrary" if emit else "parallel"),
        name="rms_proj",
    )(x, norm_g, w)
    return outs if emit else outs[0]


def _matmul_res_kernel(res_ref, *refs, n_x):
    x_refs, w_ref, o_ref, wb_ref = refs[:n_x], refs[n_x], refs[n_x + 1], refs[n_x + 2:]
    w = _bf16_weight(w_ref, wb_ref)
    acc = res_ref[...]
    k0 = 0
    for x_ref in x_refs:
        k1 = k0 + x_ref.shape[-1]
        acc = acc + jnp.dot(x_ref[...].astype(BF16), w[k0:k1, :], preferred_element_type=F32)
        k0 = k1
    o_ref[...] = acc


def _matmul_res(res, xs, w, layer, emit=False):
    n, d = res.shape
    dk = sum(x.shape[-1] for x in xs)
    assert dk == w.shape[1]
    tm = _tile(n, EMIT_TM if emit else ROW_TM)
    out_shape = [jax.ShapeDtypeStruct((n, d), F32)]
    out_specs = [pl.BlockSpec((tm, d), lambda i: (i, 0))]
    if emit:
        shape, spec = _emit_weight_out(w)
        out_shape.append(shape)
        out_specs.append(spec)
    outs = pl.pallas_call(
        functools.partial(_matmul_res_kernel, n_x=len(xs)),
        out_shape=out_shape,
        grid=(n // tm,),
        in_specs=[
            pl.BlockSpec((tm, d), lambda i: (i, 0)),
            *[pl.BlockSpec((tm, x.shape[-1]), lambda i: (i, 0)) for x in xs],
            _resident((None, dk, d), lambda i: (layer, 0, 0)),
        ],
        out_specs=out_specs,
        compiler_params=_params("arbitrary" if emit else "parallel"),
        name="matmul_res",
    )(res, *xs, w)
    return outs if emit else outs[0]


def _head_norm_swish(y, g, b):
    yc = y - jnp.mean(y, axis=-1, keepdims=True)
    var = jnp.mean(yc * yc, axis=-1, keepdims=True)
    return _silu(yc * lax.rsqrt(var + EPS) * g + b)


def _mix_prompt_kernel(h_ref, mg_ref, wi_ref, pw_ref, ps_ref, cw_ref, cb_ref, ng_ref, nb_ref, wo_ref,
                       o_ref, pst_ref, cst_ref, xn_ref, pext, pool_out, *slabs, ts, conv_hist):
    t = pl.program_id(1)
    d_pool = pext.shape[-1]
    pool_group = d_pool // len(POOL_WINDOWS)
    heads = len(slabs) // 2
    gs, ys = slabs[:heads], slabs[heads:]
    hd = gs[0].shape[-1]
    d_conv = heads * hd
    conv_width = conv_hist + 1
    ph = pext.shape[0] - ts
    gh = gs[0].shape[0] - ts

    @pl.when(t == 0)
    def _():
        pext[0:ph, :] = jnp.zeros((ph, d_pool), F32)
        for c in range(heads):
            gs[c][0:gh, :] = jnp.zeros((gh, hd), F32)

    xn_ref[...] = _rms(h_ref[...], mg_ref[...]).astype(BF16)
    pext[ph:ph + ts, :] = jnp.dot(xn_ref[...], wi_ref[:, 0:d_pool], preferred_element_type=F32)

    pos = t * ts + lax.broadcasted_iota(jnp.int32, (ts, 1), 0)
    for g, w in enumerate(POOL_WINDOWS):
        sl = slice(g * pool_group, (g + 1) * pool_group)
        cur = pext[ph:ph + ts, sl]
        acc = cur
        for k in range(1, w):
            acc = acc + pext[ph - k:ph - k + ts, sl]
        cnt = jnp.minimum(pos + 1, w).astype(F32)
        dlt = (acc / cnt - cur).astype(BF16)
        y = jnp.dot(dlt, pw_ref[g], preferred_element_type=F32) * ps_ref[:, sl]
        pool_out[:, sl] = y.astype(BF16)

    groups = CONV_CHUNK // (2 * SUBLANE)

    def conv_head(c):
        hs = slice(c * hd, (c + 1) * hd)
        bias = jnp.broadcast_to(cb_ref[:, hs], (SUBLANE, hd))
        for r0 in range(0, ts, CONV_CHUNK):
            even = [bias] * groups
            odd = [bias] * groups
            for j in range(conv_width + 1):
                rows = [gs[c][pl.ds(r0 + q * 2 * SUBLANE + (gh - conv_hist) + j, SUBLANE, stride=2), :]
                        for q in range(groups)]
                if j < conv_width:
                    wk = cw_ref[j, c]
                    even = [a + x * wk for a, x in zip(even, rows)]
                if j >= 1:
                    wk = cw_ref[j - 1, c]
                    odd = [a + x * wk for a, x in zip(odd, rows)]
            for q in range(groups):
                for phase, acc in enumerate((even[q], odd[q])):
                    y = _head_norm_swish(acc, ng_ref[:, hs], nb_ref[:, hs])
                    ys[c][pl.ds(r0 + q * 2 * SUBLANE + phase, SUBLANE, stride=2), :] = y

    def glu_pair(c0):
        lo = d_pool + c0 * hd
        xn = xn_ref[...]
        a = jnp.dot(xn, wi_ref[:, lo:lo + 2 * hd], preferred_element_type=F32)
        b = jnp.dot(xn, wi_ref[:, lo + d_conv:lo + d_conv + 2 * hd], preferred_element_type=F32)
        glu = a * jax.nn.sigmoid(b)
        gs[c0][gh:gh + ts, :] = glu[:, 0:hd]
        gs[c0 + 1][gh:gh + ts, :] = glu[:, hd:2 * hd]

    o_ref[...] = h_ref[...] + jnp.dot(pool_out[...], wo_ref[0:d_pool, :], preferred_element_type=F32)
    glu_pair(0)
    for c0 in range(0, heads, 2):
        if c0 + 2 < heads:
            glu_pair(c0 + 2)
        conv_head(c0)
        conv_head(c0 + 1)
        pair = jnp.concatenate([ys[c0][...], ys[c0 + 1][...]], axis=1).astype(BF16)
        lo = d_pool + c0 * hd
        o_ref[...] += jnp.dot(pair, wo_ref[lo:lo + 2 * hd, :], preferred_element_type=F32)

    pst_ref[...] = pext[ts:ts + ph, :]
    pext[0:ph, :] = pext[ts:ts + ph, :]
    for c in range(heads):
        cst_ref[:, c * hd:(c + 1) * hd] = gs[c][ts:ts + gh, :]
        gs[c][0:gh, :] = gs[c][ts:ts + gh, :]


def _mix_prompt(h, batch, seq, mix_norm, w_in, pool_w, pool_scale, conv_wb, conv_b, norm_g, norm_b, w_out, layer,
                w_out_layer, pool_hist, conv_hist):
    n, d = h.shape
    d_pool = pool_scale.shape[-1]
    heads, hd = conv_wb.shape[2], conv_wb.shape[4]
    d_conv = heads * hd
    assert hd == LANE and w_in.shape[-1] == d_pool + 2 * d_conv
    ts = _tile(seq, MIX_TS)
    assert ts % CONV_CHUNK == 0
    nt = seq // ts
    ph = -(-pool_hist // SUBLANE) * SUBLANE
    gh = -(-conv_hist // SUBLANE) * SUBLANE
    row = lambda b, t: (b * nt + t, 0)
    vec = lambda b, t: (layer, 0, 0)
    out, pool_tail, conv_tail = pl.pallas_call(
        functools.partial(_mix_prompt_kernel, ts=ts, conv_hist=conv_hist),
        out_shape=(jax.ShapeDtypeStruct((n, d), F32), jax.ShapeDtypeStruct((batch, ph, d_pool), F32),
                   jax.ShapeDtypeStruct((batch, gh, d_conv), F32)),
        grid=(batch, nt),
        in_specs=[
            pl.BlockSpec((ts, d), row),
            pl.BlockSpec((None, 1, d), vec),
            _resident((None, d, d_pool + 2 * d_conv), vec),
            _resident((None,) + pool_w.shape[1:], lambda b, t: (layer, 0, 0, 0)),
            pl.BlockSpec((None, 1, d_pool), vec),
            _resident((None,) + conv_wb.shape[1:], lambda b, t: (layer, 0, 0, 0, 0)),
            pl.BlockSpec((None, 1, d_conv), vec),
            pl.BlockSpec((None, 1, d_conv), vec),
            pl.BlockSpec((None, 1, d_conv), vec),
            _resident((None, d_pool + d_conv, d), lambda b, t: (w_out_layer, 0, 0)),
        ],
        out_specs=(pl.BlockSpec((ts, d), row),
                   pl.BlockSpec((None, ph, d_pool), lambda b, t: (b, 0, 0)),
                   pl.BlockSpec((None, gh, d_conv), lambda b, t: (b, 0, 0))),
        scratch_shapes=[
            pltpu.VMEM((ts, d), BF16),
            pltpu.VMEM((ph + ts, d_pool), F32),
            pltpu.VMEM((ts, d_pool), BF16),
            *[pltpu.VMEM((gh + ts, hd), F32) for _ in range(heads)],
            *[pltpu.VMEM((ts, hd), F32) for _ in range(heads)],
        ],
        compiler_params=_params("arbitrary", "arbitrary", vmem_limit=MIX_VMEM_LIMIT),
        name="mix_prompt",
    )(h, mix_norm, w_in, pool_w, pool_scale, conv_wb, conv_b, norm_g, norm_b, w_out)
    return out, pool_tail[:, ph - pool_hist:], conv_tail[:, gh - conv_hist:]


def _mix_sample_kernel(sp_ref, p_ref, sc_ref, glu_ref, pw_ref, ps_ref, cw_ref, cb_ref, ng_ref, nb_ref,
                       po_ref, co_ref, pn_ref, cn_ref, pe, ge, dbuf, *, steps, pool_hist, conv_hist):
    g = pl.program_id(0)
    _, bs, cb = pe.shape
    head_dim = ng_ref.shape[-1] * len(POOL_WINDOWS) // CONV_HEADS
    conv_width = conv_hist + 1

    for r in range(pool_hist):
        pe[r] = sp_ref[:, r, :]
    for r in range(conv_hist):
        ge[r] = sc_ref[:, r, :]
    for t in range(steps):
        pe[pool_hist + t] = p_ref[:, t, :]
        ge[conv_hist + t] = glu_ref[:, t, :]

    for t in range(steps):
        cur = pe[pool_hist + t]
        acc = cur
        dlt = jnp.zeros_like(cur)
        k = 1
        for gi, w in enumerate(POOL_WINDOWS):
            while k < w:
                acc = acc + pe[pool_hist + t - k]
                k += 1
            cnt = float(min(PAST_LEN + t + 1, w))
            dlt = jnp.where(g == gi, acc / cnt - cur, dlt)
        dbuf[t] = dlt.astype(BF16)

        y = jnp.broadcast_to(cb_ref[...], (bs, cb))
        for k in range(conv_width):
            y = y + ge[t + k] * cw_ref[k:k + 1, :]
        for hh in range(cb // head_dim):
            hs = slice(hh * head_dim, (hh + 1) * head_dim)
            co_ref[:, t, hs] = _head_norm_swish(y[:, hs], ng_ref[:, hs], nb_ref[:, hs])

    dall = dbuf[...].reshape(steps * bs, cb)
    y = jnp.dot(dall, pw_ref[...], preferred_element_type=F32) * ps_ref[...]
    for t in range(steps):
        po_ref[:, t, :] = y[t * bs:(t + 1) * bs, :]

    for i in range(pool_hist):
        pn_ref[:, i, :] = pe[i + steps]
    for i in range(conv_hist):
        cn_ref[:, i, :] = ge[i + steps]


def _mix_sample(state_pool, p, state_conv, glu, pool_w, pool_scale, conv_w, conv_b, norm_g, norm_b, layer):
    _, bs, pool_hist, d_pool = state_pool.shape
    _, _, conv_hist, d_conv = state_conv.shape
    steps = p.shape[1]
    ng = len(POOL_WINDOWS)
    cb = d_pool // ng
    assert d_conv // ng == cb and cb % (d_conv // CONV_HEADS) == 0
    vec = lambda g: (layer, 0, g)
    seq = lambda rows: pl.BlockSpec((bs, rows, cb), lambda g: (0, 0, g))
    return pl.pallas_call(
        functools.partial(_mix_sample_kernel, steps=steps, pool_hist=pool_hist, conv_hist=conv_hist),
        out_shape=(jax.ShapeDtypeStruct((bs, steps, d_pool), F32), jax.ShapeDtypeStruct((bs, steps, d_conv), F32),
                   jax.ShapeDtypeStruct((bs, pool_hist, d_pool), F32),
                   jax.ShapeDtypeStruct((bs, conv_hist, d_conv), F32)),
        grid=(ng,),
        in_specs=[
            pl.BlockSpec((None, bs, pool_hist, cb), lambda g: (layer, 0, 0, g)),
            seq(steps),
            pl.BlockSpec((None, bs, conv_hist, cb), lambda g: (layer, 0, 0, g)),
            seq(steps),
            pl.BlockSpec((None, None, cb, cb), lambda g: (layer, g, 0, 0)),
            pl.BlockSpec((None, 1, cb), vec),
            pl.BlockSpec((None, conv_hist + 1, cb), vec),
            pl.BlockSpec((None, 1, cb), vec),
            pl.BlockSpec((None, 1, cb), vec),
            pl.BlockSpec((None, 1, cb), vec),
        ],
        out_specs=(seq(steps), seq(steps), seq(pool_hist), seq(conv_hist)),
        scratch_shapes=[pltpu.VMEM((pool_hist + steps, bs, cb), F32), pltpu.VMEM((conv_hist + steps, bs, cb), F32),
                        pltpu.VMEM((steps, bs, cb), BF16)],
        compiler_params=_params("parallel"),
        name="mix_sample",
    )(state_pool, p, state_conv, glu, pool_w, pool_scale, conv_w, conv_b, norm_g, norm_b)


def _softmax(s):
    e = jnp.exp(s - jnp.max(s, axis=-1, keepdims=True))
    return e / jnp.sum(e, axis=-1, keepdims=True)


def _attn_prompt_kernel(h_ref, g_ref, wq_ref, k_ref, v_ref, wo_ref, o_ref, *, heads):
    x = h_ref[...]
    d = x.shape[-1]
    hd = d // heads
    q = jnp.dot(_rms(x, g_ref[...]).astype(BF16), wq_ref[...], preferred_element_type=F32)
    acc = x
    for hh in range(heads):
        hs = slice(hh * hd, (hh + 1) * hd)
        s = lax.dot_general(q[:, hs].astype(BF16), k_ref[:, hs].astype(BF16),
                            (((1,), (1,)), ((), ())), preferred_element_type=F32) * (hd ** -0.5)
        p = _softmax(s).astype(BF16)
        oh = jnp.dot(p, v_ref[:, hs].astype(BF16), preferred_element_type=F32)
        acc = acc + jnp.dot(oh.astype(BF16), wo_ref[hs, :], preferred_element_type=F32)
    o_ref[...] = acc


def _attn_prompt(h, batch, seq, norm_g, w_q, mk, mv, w_o, layer, w_layer, heads):
    n, d = h.shape
    n_mem = mk.shape[1]
    tq = _tile(seq, ROW_TM)
    nt = seq // tq
    row = lambda b, t: (b * nt + t, 0)
    return pl.pallas_call(
        functools.partial(_attn_prompt_kernel, heads=heads),
        out_shape=jax.ShapeDtypeStruct((n, d), F32),
        grid=(batch, nt),
        in_specs=[
            pl.BlockSpec((tq, d), row),
            pl.BlockSpec((None, 1, d), lambda b, t: (layer, 0, 0)),
            _resident((None, d, d), lambda b, t: (w_layer, 0, 0)),
            pl.BlockSpec((None, n_mem, d), lambda b, t: (b, 0, 0)),
            pl.BlockSpec((None, n_mem, d), lambda b, t: (b, 0, 0)),
            _resident((None, d, d), lambda b, t: (w_layer, 0, 0)),
        ],
        out_specs=pl.BlockSpec((tq, d), row),
        compiler_params=_params("parallel", "parallel"),
        name="attn_prompt",
    )(h, norm_g, w_q, mk, mv, w_o)


def _attn_sample_kernel(q_ref, k_ref, v_ref, o_ref):
    bb, t, _ = q_ref.shape
    _, n_mem, heads, hd = k_ref.shape
    q = q_ref[...]
    qh = jnp.concatenate([q[:, :, h * hd:(h + 1) * hd] for h in range(heads)], axis=1).astype(BF16)
    k = k_ref[...].reshape(bb, n_mem * heads, hd).astype(BF16)
    v = v_ref[...].reshape(bb, n_mem * heads, hd).astype(BF16)
    s = jnp.einsum("bqd,bkd->bqk", qh, k, preferred_element_type=F32) * (hd ** -0.5)
    q_head = lax.broadcasted_iota(jnp.int32, s.shape, 1) // t
    k_head = lax.broadcasted_iota(jnp.int32, s.shape, 2) % heads
    p = _softmax(jnp.where(q_head == k_head, s, -jnp.inf)).astype(BF16)
    o = jnp.einsum("bqk,bkd->bqd", p, v, preferred_element_type=F32)
    for h in range(heads):
        o_ref[:, :, h * hd:(h + 1) * hd] = o[:, h * t:(h + 1) * t, :]


def _attn_sample(q, cache_k, cache_v, layer):
    bs, t, d = q.shape
    _, _, n_mem, heads, hd = cache_k.shape
    bb = _tile(bs, SAMPLE_ATTN_BB)
    kv = pl.BlockSpec((None, bb, n_mem, heads, hd), lambda i: (layer, i, 0, 0, 0))
    return pl.pallas_call(
        _attn_sample_kernel,
        out_shape=jax.ShapeDtypeStruct((bs, t, d), F32),
        grid=(bs // bb,),
        in_specs=[pl.BlockSpec((bb, t, d), lambda i: (i, 0, 0)), kv, kv],
        out_specs=pl.BlockSpec((bb, t, d), lambda i: (i, 0, 0)),
        compiler_params=_params("parallel"),
        name="attn_sample",
    )(q, cache_k, cache_v)


def kernel(x_prompt, x_sample, state_pool, state_conv, cache_mem_k, cache_mem_v, mem_prompt, ffn1_norm, ffn1_w_gate, ffn1_w_up, ffn1_w_down, mix_norm, w_in, pool_w, pool_scale, conv_w, conv_b, conv_norm_g, conv_norm_b, w_out, xattn_norm, mem_norm, w_q, w_mk, w_mv, w_o, ffn2_norm, ffn2_w_gate, ffn2_w_up, ffn2_w_down, final_norm):
    batch, seq, d = x_prompt.shape
    bs, steps, _ = x_sample.shape
    depth = state_pool.shape[0]
    pool_hist, d_pool = state_pool.shape[2:]
    conv_hist, d_conv = state_conv.shape[2:]
    n_mem, heads, head_dim = cache_mem_k.shape[2:]

    bf = lambda w: w.astype(BF16)
    ffn_w = ((ffn1_w_gate, ffn1_w_up, ffn1_w_down), (ffn2_w_gate, ffn2_w_up, ffn2_w_down))
    w_in_b, pool_w_b = bf(w_in), bf(pool_w)

    vec = lambda v: v.reshape(v.shape[0], 1, v.shape[1])
    ffn_norms = (vec(ffn1_norm), vec(ffn2_norm))
    mix_norm_v, xattn_norm_v, mem_norm_v = vec(mix_norm), vec(xattn_norm), vec(mem_norm)
    pool_scale_v, conv_b_v, norm_g_v, norm_b_v = vec(pool_scale), vec(conv_b), vec(conv_norm_g), vec(conv_norm_b)
    final_g = final_norm.reshape(1, d)
    conv_head_dim = d_conv // CONV_HEADS
    conv_wb = jnp.broadcast_to(conv_w.reshape(depth, conv_hist + 1, CONV_HEADS, 1, conv_head_dim),
                               (depth, conv_hist + 1, CONV_HEADS, SUBLANE, conv_head_dim))

    mem = mem_prompt.reshape(batch * n_mem, d)

    hp = x_prompt.reshape(batch * seq, d)
    hs = x_sample.reshape(bs * steps, d)
    pool_p, conv_p, mk_p, mv_p, pool_s, conv_s = [], [], [], [], [], []

    for l in range(depth):
        last = l == depth - 1
        hs, *ffn_wb = _ffn(hs, ffn_norms[0], *ffn_w[0], l)
        hp = _ffn(hp, ffn_norms[0], *ffn_wb, l)

        p, glu = _proj_in(hs, mix_norm_v, w_in_b, l, d_pool, d_conv)
        po, co, pool_new, conv_new = _mix_sample(state_pool, p.reshape(bs, steps, d_pool), state_conv,
                                                 glu.reshape(bs, steps, d_conv), pool_w_b, pool_scale_v, conv_w,
                                                 conv_b_v, norm_g_v, norm_b_v, l)
        pool_s.append(pool_new)
        conv_s.append(conv_new)
        hs, w_out_b = _matmul_res(hs, [po.reshape(bs * steps, d_pool), co.reshape(bs * steps, d_conv)], w_out, l,
                                  emit=True)
        hp, pool_tail, conv_tail = _mix_prompt(hp, batch, seq, mix_norm_v, w_in_b, pool_w_b, pool_scale_v, conv_wb,
                                               conv_b_v, norm_g_v, norm_b_v, w_out_b, l, 0, pool_hist, conv_hist)
        pool_p.append(pool_tail)
        conv_p.append(conv_tail)

        q, w_q_b = _rms_proj(hs, xattn_norm_v, w_q, l, emit=True)
        o = _attn_sample(q.reshape(bs, steps, d), cache_mem_k, cache_mem_v, l)
        hs, w_o_b = _matmul_res(hs, [o.reshape(bs * steps, d)], w_o, l, emit=True)
        mk = _rms_proj(mem, mem_norm_v, w_mk, l)
        mv = _rms_proj(mem, mem_norm_v, w_mv, l)
        mk_p.append(mk.reshape(batch, n_mem, heads, head_dim))
        mv_p.append(mv.reshape(batch, n_mem, heads, head_dim))
        hp = _attn_prompt(hp, batch, seq, xattn_norm_v, w_q_b, mk.reshape(batch, n_mem, d),
                          mv.reshape(batch, n_mem, d), w_o_b, l, 0, heads)

        hs, *ffn_wb = _ffn(hs, ffn_norms[1], *ffn_w[1], l, final_g if last else None)
        hp = _ffn(hp, ffn_norms[1], *ffn_wb, l, final_g if last else None)

    return (hp.reshape(batch, seq, d), hs.reshape(bs, steps, d), jnp.stack(pool_p), jnp.stack(conv_p),
            jnp.stack(mk_p), jnp.stack(mv_p), jnp.stack(pool_s), jnp.stack(conv_s))
```

```python
import functools

import jax
import jax.numpy as jnp
from jax import lax
from jax.experimental import pallas as pl
from jax.experimental.pallas import tpu as pltpu

F32 = jnp.float32
BF16 = jnp.bfloat16

EPS = 1e-6
PAST_LEN = 16384
POOL_WINDOWS = (2, 4, 8, 16)
CONV_HEADS = 8

LANE = 128
SUBLANE = 8
VMEM_LIMIT = 56 * 1024 * 1024

FFN_TM = 1024
FFN_TF = 1024
FFN_SUB = 256
FFN_EMIT_TF = 256
FFN_VMEM_LIMIT = 60 * 1024 * 1024
MIX_VMEM_LIMIT = 60 * 1024 * 1024
ROW_TM = 512
EMIT_TM = 256
MIX_TS = 512
CONV_CHUNK = 32
SAMPLE_ATTN_BB = 4


def _params(*sem, vmem_limit=VMEM_LIMIT):
    return pltpu.CompilerParams(dimension_semantics=sem, vmem_limit_bytes=vmem_limit)


def _resident(block_shape, index_map):
    return pl.BlockSpec(block_shape, index_map, pipeline_mode=pl.Buffered(1))


def _rms(x, g):
    return x * lax.rsqrt(jnp.mean(x * x, axis=-1, keepdims=True) + EPS) * g


def _silu(x):
    return x * jax.nn.sigmoid(x)


def _tile(n, t):
    t = min(n, t)
    assert n % t == 0, (n, t)
    return t


def _ffn_kernel(x_ref, g_ref, wg_ref, wu_ref, wd_ref, fg_ref, o_ref, *rest, final_norm, d_ff, sub, emit):
    if emit:
        wg_o, wu_o, wd_o, xn_ref = rest
    else:
        (xn_ref,) = rest
    j = pl.program_id(1)
    tf = wg_ref.shape[-1]
    nj = pl.cdiv(d_ff, tf)

    @pl.when(j == 0)
    def _():
        xn_ref[...] = _rms(x_ref[...], g_ref[...]).astype(BF16)
        o_ref[...] = jnp.zeros_like(o_ref)

    def columns(lo, n):
        if emit:
            wg_o[:, 0:n] = wg_ref[0, :, lo:lo + n].astype(BF16)
            wu_o[:, 0:n] = wu_ref[0, :, lo:lo + n].astype(BF16)
            wd_o[0:n, :] = wd_ref[0, lo:lo + n, :].astype(BF16)
            wg, wu, wd, lo = wg_o, wu_o, wd_o, 0
        else:
            wg, wu, wd = wg_ref, wu_ref, wd_ref
        xn = xn_ref[...]
        for c0 in range(lo, lo + n, sub):
            c1 = min(c0 + sub, lo + n)
            gate = jnp.dot(xn, wg[:, c0:c1], preferred_element_type=F32)
            up = jnp.dot(xn, wu[:, c0:c1], preferred_element_type=F32)
            hmid = (_silu(gate) * up).astype(BF16)
            o_ref[...] += jnp.dot(hmid, wd[c0:c1, :], preferred_element_type=F32)

    done = nj * tf - d_ff
    if done == 0:
        columns(0, tf)
    else:
        pl.when(j < nj - 1)(lambda: columns(0, tf))
        pl.when(j == nj - 1)(lambda: columns(done, tf - done))

    @pl.when(j == nj - 1)
    def _():
        y = x_ref[...] + 0.5 * o_ref[...]
        if final_norm:
            y = _rms(y, fg_ref[...])
        o_ref[...] = y


def _ffn(x, norm_g, wg, wu, wd, layer, final_g=None):
    n, d = x.shape
    d_ff = wg.shape[-1]
    emit = wg.ndim == 3
    tm = _tile(n, FFN_TM)
    tf, sub = (FFN_EMIT_TF, FFN_EMIT_TF) if emit else (FFN_TF, FFN_SUB)
    tf = min(tf, d_ff)
    assert tf % LANE == 0 and d_ff % LANE == 0
    final_norm = final_g is not None
    if final_g is None:
        final_g = jnp.ones((1, d), F32)
    start = lambda j: pl.multiple_of(jnp.minimum(j * tf, d_ff - tf), LANE)
    el = pl.Element
    bf16_specs = [
        pl.BlockSpec((el(d), el(tf)), lambda i, j: (0, start(j))),
        pl.BlockSpec((el(d), el(tf)), lambda i, j: (0, start(j))),
        pl.BlockSpec((el(tf), el(d)), lambda i, j: (start(j), 0)),
    ]
    f32_specs = [
        pl.BlockSpec((el(1), el(d), el(tf)), lambda i, j: (layer, 0, start(j))),
        pl.BlockSpec((el(1), el(d), el(tf)), lambda i, j: (layer, 0, start(j))),
        pl.BlockSpec((el(1), el(tf), el(d)), lambda i, j: (layer, start(j), 0)),
    ]
    out_shape = [jax.ShapeDtypeStruct((n, d), F32)]
    out_specs = [pl.BlockSpec((tm, d), lambda i, j: (i, 0))]
    if emit:
        assert n == tm, "the emitted casts are written by a single row tile"
        out_shape += [jax.ShapeDtypeStruct((d, d_ff), BF16), jax.ShapeDtypeStruct((d, d_ff), BF16),
                      jax.ShapeDtypeStruct((d_ff, d), BF16)]
        out_specs += [pl.BlockSpec((d, tf), lambda i, j: (0, j)), pl.BlockSpec((d, tf), lambda i, j: (0, j)),
                      pl.BlockSpec((tf, d), lambda i, j: (j, 0))]
    outs = pl.pallas_call(
        functools.partial(_ffn_kernel, final_norm=final_norm, d_ff=d_ff, sub=sub, emit=emit),
        out_shape=out_shape,
        grid=(n // tm, pl.cdiv(d_ff, tf)),
        in_specs=[
            _resident((tm, d), lambda i, j: (i, 0)),
            pl.BlockSpec((None, 1, d), lambda i, j: (layer, 0, 0)),
            *(f32_specs if emit else bf16_specs),
            pl.BlockSpec((1, d), lambda i, j: (0, 0)),
        ],
        out_specs=out_specs,
        scratch_shapes=[pltpu.VMEM((tm, d), BF16)],
        compiler_params=_params("parallel", "arbitrary", vmem_limit=FFN_VMEM_LIMIT),
        name="ffn_emit" if emit else "ffn",
    )(x, norm_g, wg, wu, wd, final_g)
    return outs if emit else outs[0]


def _proj_in_kernel(x_ref, g_ref, w_ref, p_ref, glu_ref, *, d_pool, d_conv):
    xn = _rms(x_ref[...], g_ref[...]).astype(BF16)
    z = jnp.dot(xn, w_ref[...], preferred_element_type=F32)
    p_ref[...] = z[:, :d_pool]
    glu_ref[...] = z[:, d_pool:d_pool + d_conv] * jax.nn.sigmoid(z[:, d_pool + d_conv:])


def _proj_in(x, norm_g, w_in, layer, d_pool, d_conv):
    n, d = x.shape
    tm = _tile(n, ROW_TM)
    dz = w_in.shape[-1]
    return pl.pallas_call(
        functools.partial(_proj_in_kernel, d_pool=d_pool, d_conv=d_conv),
        out_shape=(jax.ShapeDtypeStruct((n, d_pool), F32), jax.ShapeDtypeStruct((n, d_conv), F32)),
        grid=(n // tm,),
        in_specs=[
            pl.BlockSpec((tm, d), lambda i: (i, 0)),
            pl.BlockSpec((None, 1, d), lambda i: (layer, 0, 0)),
            _resident((None, d, dz), lambda i: (layer, 0, 0)),
        ],
        out_specs=(pl.BlockSpec((tm, d_pool), lambda i: (i, 0)),
                   pl.BlockSpec((tm, d_conv), lambda i: (i, 0))),
        compiler_params=_params("parallel"),
        name="proj_in",
    )(x, norm_g, w_in)


def _bf16_weight(w_ref, wb_ref):
    if not wb_ref:
        return w_ref[...].astype(BF16)
    (wb_ref,) = wb_ref

    @pl.when(pl.program_id(0) == 0)
    def _():
        wb_ref[...] = w_ref[...].astype(BF16)

    return wb_ref[...]


def _emit_weight_out(w):
    return (jax.ShapeDtypeStruct((1,) + w.shape[1:], BF16),
            pl.BlockSpec((None,) + w.shape[1:], lambda i: (0, 0, 0)))


def _rms_proj_kernel(x_ref, g_ref, w_ref, o_ref, *wb_ref):
    xn = _rms(x_ref[...], g_ref[...]).astype(BF16)
    o_ref[...] = jnp.dot(xn, _bf16_weight(w_ref, wb_ref), preferred_element_type=F32)


def _rms_proj(x, norm_g, w, layer, emit=False):
    n, d = x.shape
    dn = w.shape[-1]
    tm = _tile(n, EMIT_TM if emit else ROW_TM)
    out_shape = [jax.ShapeDtypeStruct((n, dn), F32)]
    out_specs = [pl.BlockSpec((tm, dn), lambda i: (i, 0))]
    if emit:
        shape, spec = _emit_weight_out(w)
        out_shape.append(shape)
        out_specs.append(spec)
    outs = pl.pallas_call(
        _rms_proj_kernel,
        out_shape=out_shape,
        grid=(n // tm,),
        in_specs=[
            pl.BlockSpec((tm, d), lambda i: (i, 0)),
            pl.BlockSpec((None, 1, d), lambda i: (layer, 0, 0)),
            _resident((None, d, dn), lambda i: (layer, 0, 0)),
        ],
        out_specs=out_specs,
        compiler_params=_params("arbitrary" if emit else "parallel"),
        name="rms_proj",
    )(x, norm_g, w)
    return outs if emit else outs[0]


def _matmul_res_kernel(res_ref, *refs, n_x):
    x_refs, w_ref, o_ref, wb_ref = refs[:n_x], refs[n_x], refs[n_x + 1], refs[n_x + 2:]
    w = _bf16_weight(w_ref, wb_ref)
    acc = res_ref[...]
    k0 = 0
    for x_ref in x_refs:
        k1 = k0 + x_ref.shape[-1]
        acc = acc + jnp.dot(x_ref[...].astype(BF16), w[k0:k1, :], preferred_element_type=F32)
        k0 = k1
    o_ref[...] = acc


def _matmul_res(res, xs, w, layer, emit=False):
    n, d = res.shape
    dk = sum(x.shape[-1] for x in xs)
    assert dk == w.shape[1]
    tm = _tile(n, EMIT_TM if emit else ROW_TM)
    out_shape = [jax.ShapeDtypeStruct((n, d), F32)]
    out_specs = [pl.BlockSpec((tm, d), lambda i: (i, 0))]
    if emit:
        shape, spec = _emit_weight_out(w)
        out_shape.append(shape)
        out_specs.append(spec)
    outs = pl.pallas_call(
        functools.partial(_matmul_res_kernel, n_x=len(xs)),
        out_shape=out_shape,
        grid=(n // tm,),
        in_specs=[
            pl.BlockSpec((tm, d), lambda i: (i, 0)),
            *[pl.BlockSpec((tm, x.shape[-1]), lambda i: (i, 0)) for x in xs],
            _resident((None, dk, d), lambda i: (layer, 0, 0)),
        ],
        out_specs=out_specs,
        compiler_params=_params("arbitrary" if emit else "parallel"),
        name="matmul_res",
    )(res, *xs, w)
    return outs if emit else outs[0]


def _head_norm_swish(y, g, b):
    yc = y - jnp.mean(y, axis=-1, keepdims=True)
    var = jnp.mean(yc * yc, axis=-1, keepdims=True)
    return _silu(yc * lax.rsqrt(var + EPS) * g + b)


def _mix_prompt_kernel(h_ref, mg_ref, wi_ref, pw_ref, ps_ref, cw_ref, cb_ref, ng_ref, nb_ref, wo_ref,
                       o_ref, pst_ref, cst_ref, xn_ref, pext, pool_out, *slabs, ts, conv_hist):
    t = pl.program_id(1)
    d_pool = pext.shape[-1]
    pool_group = d_pool // len(POOL_WINDOWS)
    heads = len(slabs) // 2
    gs, ys = slabs[:heads], slabs[heads:]
    hd = gs[0].shape[-1]
    d_conv = heads * hd
    conv_width = conv_hist + 1
    ph = pext.shape[0] - ts
    gh = gs[0].shape[0] - ts

    @pl.when(t == 0)
    def _():
        pext[0:ph, :] = jnp.zeros((ph, d_pool), F32)
        for c in range(heads):
            gs[c][0:gh, :] = jnp.zeros((gh, hd), F32)

    xn_ref[...] = _rms(h_ref[...], mg_ref[...]).astype(BF16)
    pext[ph:ph + ts, :] = jnp.dot(xn_ref[...], wi_ref[:, 0:d_pool], preferred_element_type=F32)

    pos = t * ts + lax.broadcasted_iota(jnp.int32, (ts, 1), 0)
    for g, w in enumerate(POOL_WINDOWS):
        sl = slice(g * pool_group, (g + 1) * pool_group)
        cur = pext[ph:ph + ts, sl]
        acc = cur
        for k in range(1, w):
            acc = acc + pext[ph - k:ph - k + ts, sl]
        cnt = jnp.minimum(pos + 1, w).astype(F32)
        dlt = (acc / cnt - cur).astype(BF16)
        y = jnp.dot(dlt, pw_ref[g], preferred_element_type=F32) * ps_ref[:, sl]
        pool_out[:, sl] = y.astype(BF16)

    groups = CONV_CHUNK // (2 * SUBLANE)

    def conv_head(c):
        hs = slice(c * hd, (c + 1) * hd)
        bias = jnp.broadcast_to(cb_ref[:, hs], (SUBLANE, hd))
        for r0 in range(0, ts, CONV_CHUNK):
            even = [bias] * groups
            odd = [bias] * groups
            for j in range(conv_width + 1):
                rows = [gs[c][pl.ds(r0 + q * 2 * SUBLANE + (gh - conv_hist) + j, SUBLANE, stride=2), :]
                        for q in range(groups)]
                if j < conv_width:
                    wk = cw_ref[j, c]
                    even = [a + x * wk for a, x in zip(even, rows)]
                if j >= 1:
                    wk = cw_ref[j - 1, c]
                    odd = [a + x * wk for a, x in zip(odd, rows)]
            for q in range(groups):
                for phase, acc in enumerate((even[q], odd[q])):
                    y = _head_norm_swish(acc, ng_ref[:, hs], nb_ref[:, hs])
                    ys[c][pl.ds(r0 + q * 2 * SUBLANE + phase, SUBLANE, stride=2), :] = y

    def glu_pair(c0):
        lo = d_pool + c0 * hd
        xn = xn_ref[...]
        a = jnp.dot(xn, wi_ref[:, lo:lo + 2 * hd], preferred_element_type=F32)
        b = jnp.dot(xn, wi_ref[:, lo + d_conv:lo + d_conv + 2 * hd], preferred_element_type=F32)
        glu = a * jax.nn.sigmoid(b)
        gs[c0][gh:gh + ts, :] = glu[:, 0:hd]
        gs[c0 + 1][gh:gh + ts, :] = glu[:, hd:2 * hd]

    o_ref[...] = h_ref[...] + jnp.dot(pool_out[...], wo_ref[0:d_pool, :], preferred_element_type=F32)
    glu_pair(0)
    for c0 in range(0, heads, 2):
        if c0 + 2 < heads:
            glu_pair(c0 + 2)
        conv_head(c0)
        conv_head(c0 + 1)
        pair = jnp.concatenate([ys[c0][...], ys[c0 + 1][...]], axis=1).astype(BF16)
        lo = d_pool + c0 * hd
        o_ref[...] += jnp.dot(pair, wo_ref[lo:lo + 2 * hd, :], preferred_element_type=F32)

    pst_ref[...] = pext[ts:ts + ph, :]
    pext[0:ph, :] = pext[ts:ts + ph, :]
    for c in range(heads):
        cst_ref[:, c * hd:(c + 1) * hd] = gs[c][ts:ts + gh, :]
        gs[c][0:gh, :] = gs[c][ts:ts + gh, :]


def _mix_prompt(h, batch, seq, mix_norm, w_in, pool_w, pool_scale, conv_wb, conv_b, norm_g, norm_b, w_out, layer,
                w_out_layer, pool_hist, conv_hist):
    n, d = h.shape
    d_pool = pool_scale.shape[-1]
    heads, hd = conv_wb.shape[2], conv_wb.shape[4]
    d_conv = heads * hd
    assert hd == LANE and w_in.shape[-1] == d_pool + 2 * d_conv
    ts = _tile(seq, MIX_TS)
    assert ts % CONV_CHUNK == 0
    nt = seq // ts
    ph = -(-pool_hist // SUBLANE) * SUBLANE
    gh = -(-conv_hist // SUBLANE) * SUBLANE
    row = lambda b, t: (b * nt + t, 0)
    vec = lambda b, t: (layer, 0, 0)
    out, pool_tail, conv_tail = pl.pallas_call(
        functools.partial(_mix_prompt_kernel, ts=ts, conv_hist=conv_hist),
        out_shape=(jax.ShapeDtypeStruct((n, d), F32), jax.ShapeDtypeStruct((batch, ph, d_pool), F32),
                   jax.ShapeDtypeStruct((batch, gh, d_conv), F32)),
        grid=(batch, nt),
        in_specs=[
            pl.BlockSpec((ts, d), row),
            pl.BlockSpec((None, 1, d), vec),
            _resident((None, d, d_pool + 2 * d_conv), vec),
            _resident((None,) + pool_w.shape[1:], lambda b, t: (layer, 0, 0, 0)),
            pl.BlockSpec((None, 1, d_pool), vec),
            _resident((None,) + conv_wb.shape[1:], lambda b, t: (layer, 0, 0, 0, 0)),
            pl.BlockSpec((None, 1, d_conv), vec),
            pl.BlockSpec((None, 1, d_conv), vec),
            pl.BlockSpec((None, 1, d_conv), vec),
            _resident((None, d_pool + d_conv, d), lambda b, t: (w_out_layer, 0, 0)),
        ],
        out_specs=(pl.BlockSpec((ts, d), row),
                   pl.BlockSpec((None, ph, d_pool), lambda b, t: (b, 0, 0)),
                   pl.BlockSpec((None, gh, d_conv), lambda b, t: (b, 0, 0))),
        scratch_shapes=[
            pltpu.VMEM((ts, d), BF16),
            pltpu.VMEM((ph + ts, d_pool), F32),
            pltpu.VMEM((ts, d_pool), BF16),
            *[pltpu.VMEM((gh + ts, hd), F32) for _ in range(heads)],
            *[pltpu.VMEM((ts, hd), F32) for _ in range(heads)],
        ],
        compiler_params=_params("arbitrary", "arbitrary", vmem_limit=MIX_VMEM_LIMIT),
        name="mix_prompt",
    )(h, mix_norm, w_in, pool_w, pool_scale, conv_wb, conv_b, norm_g, norm_b, w_out)
    return out, pool_tail[:, ph - pool_hist:], conv_tail[:, gh - conv_hist:]


def _mix_sample_kernel(sp_ref, p_ref, sc_ref, glu_ref, pw_ref, ps_ref, cw_ref, cb_ref, ng_ref, nb_ref,
                       po_ref, co_ref, pn_ref, cn_ref, pe, ge, dbuf, *, steps, pool_hist, conv_hist):
    g = pl.program_id(0)
    _, bs, cb = pe.shape
    head_dim = ng_ref.shape[-1] * len(POOL_WINDOWS) // CONV_HEADS
    conv_width = conv_hist + 1

    pe[0:pool_hist] = sp_ref[...]
    ge[0:conv_hist] = sc_ref[...]
    for t in range(steps):
        pe[pool_hist + t] = p_ref[:, t, :]
        ge[conv_hist + t] = glu_ref[:, t, :]

    for t in range(steps):
        cur = pe[pool_hist + t]
        acc = cur
        dlt = jnp.zeros_like(cur)
        k = 1
        for gi, w in enumerate(POOL_WINDOWS):
            while k < w:
                acc = acc + pe[pool_hist + t - k]
                k += 1
            cnt = float(min(PAST_LEN + t + 1, w))
            dlt = jnp.where(g == gi, acc / cnt - cur, dlt)
        dbuf[t] = dlt.astype(BF16)

        y = jnp.broadcast_to(cb_ref[...], (bs, cb))
        for k in range(conv_width):
            y = y + ge[t + k] * cw_ref[k:k + 1, :]
        for hh in range(cb // head_dim):
            hs = slice(hh * head_dim, (hh + 1) * head_dim)
            co_ref[:, t, hs] = _head_norm_swish(y[:, hs], ng_ref[:, hs], nb_ref[:, hs])

    dall = dbuf[...].reshape(steps * bs, cb)
    y = jnp.dot(dall, pw_ref[...], preferred_element_type=F32) * ps_ref[...]
    for t in range(steps):
        po_ref[:, t, :] = y[t * bs:(t + 1) * bs, :]

    pn_ref[...] = pe[steps:steps + pool_hist]
    cn_ref[...] = ge[steps:steps + conv_hist]


def _mix_sample(state_pool, p, state_conv, glu, pool_w, pool_scale, conv_w, conv_b, norm_g, norm_b, layer):
    _, pool_hist, bs, d_pool = state_pool.shape
    _, conv_hist, _, d_conv = state_conv.shape
    steps = p.shape[1]
    ng = len(POOL_WINDOWS)
    cb = d_pool // ng
    assert d_conv // ng == cb and cb % (d_conv // CONV_HEADS) == 0
    vec = lambda g: (layer, 0, g)
    seq = lambda rows: pl.BlockSpec((bs, rows, cb), lambda g: (0, 0, g))
    slab = lambda rows: pl.BlockSpec((rows, bs, cb), lambda g: (0, 0, g))
    return pl.pallas_call(
        functools.partial(_mix_sample_kernel, steps=steps, pool_hist=pool_hist, conv_hist=conv_hist),
        out_shape=(jax.ShapeDtypeStruct((bs, steps, d_pool), F32), jax.ShapeDtypeStruct((bs, steps, d_conv), F32),
                   jax.ShapeDtypeStruct((pool_hist, bs, d_pool), F32),
                   jax.ShapeDtypeStruct((conv_hist, bs, d_conv), F32)),
        grid=(ng,),
        in_specs=[
            pl.BlockSpec((None, pool_hist, bs, cb), lambda g: (layer, 0, 0, g)),
            seq(steps),
            pl.BlockSpec((None, conv_hist, bs, cb), lambda g: (layer, 0, 0, g)),
            seq(steps),
            pl.BlockSpec((None, None, cb, cb), lambda g: (layer, g, 0, 0)),
            pl.BlockSpec((None, 1, cb), vec),
            pl.BlockSpec((None, conv_hist + 1, cb), vec),
            pl.BlockSpec((None, 1, cb), vec),
            pl.BlockSpec((None, 1, cb), vec),
            pl.BlockSpec((None, 1, cb), vec),
        ],
        out_specs=(seq(steps), seq(steps), slab(pool_hist), slab(conv_hist)),
        scratch_shapes=[pltpu.VMEM((pool_hist + steps, bs, cb), F32), pltpu.VMEM((conv_hist + steps, bs, cb), F32),
                        pltpu.VMEM((steps, bs, cb), BF16)],
        compiler_params=_params("parallel"),
        name="mix_sample",
    )(state_pool, p, state_conv, glu, pool_w, pool_scale, conv_w, conv_b, norm_g, norm_b)


def _softmax(s):
    e = jnp.exp(s - jnp.max(s, axis=-1, keepdims=True))
    return e / jnp.sum(e, axis=-1, keepdims=True)


def _attn_prompt_kernel(h_ref, g_ref, wq_ref, k_ref, v_ref, wo_ref, o_ref, *, heads):
    x = h_ref[...]
    d = x.shape[-1]
    hd = d // heads
    q = jnp.dot(_rms(x, g_ref[...]).astype(BF16), wq_ref[...], preferred_element_type=F32)
    acc = x
    for hh in range(heads):
        hs = slice(hh * hd, (hh + 1) * hd)
        s = lax.dot_general(q[:, hs].astype(BF16), k_ref[:, hs].astype(BF16),
                            (((1,), (1,)), ((), ())), preferred_element_type=F32) * (hd ** -0.5)
        p = _softmax(s).astype(BF16)
        oh = jnp.dot(p, v_ref[:, hs].astype(BF16), preferred_element_type=F32)
        acc = acc + jnp.dot(oh.astype(BF16), wo_ref[hs, :], preferred_element_type=F32)
    o_ref[...] = acc


def _attn_prompt(h, batch, seq, norm_g, w_q, mk, mv, w_o, layer, w_layer, heads):
    n, d = h.shape
    n_mem = mk.shape[1]
    tq = _tile(seq, ROW_TM)
    nt = seq // tq
    row = lambda b, t: (b * nt + t, 0)
    return pl.pallas_call(
        functools.partial(_attn_prompt_kernel, heads=heads),
        out_shape=jax.ShapeDtypeStruct((n, d), F32),
        grid=(batch, nt),
        in_specs=[
            pl.BlockSpec((tq, d), row),
            pl.BlockSpec((None, 1, d), lambda b, t: (layer, 0, 0)),
            _resident((None, d, d), lambda b, t: (w_layer, 0, 0)),
            pl.BlockSpec((None, n_mem, d), lambda b, t: (b, 0, 0)),
            pl.BlockSpec((None, n_mem, d), lambda b, t: (b, 0, 0)),
            _resident((None, d, d), lambda b, t: (w_layer, 0, 0)),
        ],
        out_specs=pl.BlockSpec((tq, d), row),
        compiler_params=_params("parallel", "parallel"),
        name="attn_prompt",
    )(h, norm_g, w_q, mk, mv, w_o)


def _attn_sample_kernel(q_ref, k_ref, v_ref, o_ref):
    bb, t, _ = q_ref.shape
    _, n_mem, heads, hd = k_ref.shape
    q = q_ref[...]
    qh = jnp.concatenate([q[:, :, h * hd:(h + 1) * hd] for h in range(heads)], axis=1).astype(BF16)
    k = k_ref[...].reshape(bb, n_mem * heads, hd).astype(BF16)
    v = v_ref[...].reshape(bb, n_mem * heads, hd).astype(BF16)
    s = jnp.einsum("bqd,bkd->bqk", qh, k, preferred_element_type=F32) * (hd ** -0.5)
    q_head = lax.broadcasted_iota(jnp.int32, s.shape, 1) // t
    k_head = lax.broadcasted_iota(jnp.int32, s.shape, 2) % heads
    p = _softmax(jnp.where(q_head == k_head, s, -jnp.inf)).astype(BF16)
    o = jnp.einsum("bqk,bkd->bqd", p, v, preferred_element_type=F32)
    for h in range(heads):
        o_ref[:, :, h * hd:(h + 1) * hd] = o[:, h * t:(h + 1) * t, :]


def _attn_sample(q, cache_k, cache_v, layer):
    bs, t, d = q.shape
    _, _, n_mem, heads, hd = cache_k.shape
    bb = _tile(bs, SAMPLE_ATTN_BB)
    kv = pl.BlockSpec((None, bb, n_mem, heads, hd), lambda i: (layer, i, 0, 0, 0))
    return pl.pallas_call(
        _attn_sample_kernel,
        out_shape=jax.ShapeDtypeStruct((bs, t, d), F32),
        grid=(bs // bb,),
        in_specs=[pl.BlockSpec((bb, t, d), lambda i: (i, 0, 0)), kv, kv],
        out_specs=pl.BlockSpec((bb, t, d), lambda i: (i, 0, 0)),
        compiler_params=_params("parallel"),
        name="attn_sample",
    )(q, cache_k, cache_v)


def kernel(x_prompt, x_sample, state_pool, state_conv, cache_mem_k, cache_mem_v, mem_prompt, ffn1_norm, ffn1_w_gate, ffn1_w_up, ffn1_w_down, mix_norm, w_in, pool_w, pool_scale, conv_w, conv_b, conv_norm_g, conv_norm_b, w_out, xattn_norm, mem_norm, w_q, w_mk, w_mv, w_o, ffn2_norm, ffn2_w_gate, ffn2_w_up, ffn2_w_down, final_norm):
    batch, seq, d = x_prompt.shape
    bs, steps, _ = x_sample.shape
    depth = state_pool.shape[0]
    pool_hist, d_pool = state_pool.shape[2:]
    conv_hist, d_conv = state_conv.shape[2:]
    n_mem, heads, head_dim = cache_mem_k.shape[2:]

    bf = lambda w: w.astype(BF16)
    ffn_w = ((ffn1_w_gate, ffn1_w_up, ffn1_w_down), (ffn2_w_gate, ffn2_w_up, ffn2_w_down))
    w_in_b, pool_w_b = bf(w_in), bf(pool_w)

    vec = lambda v: v.reshape(v.shape[0], 1, v.shape[1])
    ffn_norms = (vec(ffn1_norm), vec(ffn2_norm))
    mix_norm_v, xattn_norm_v, mem_norm_v = vec(mix_norm), vec(xattn_norm), vec(mem_norm)
    pool_scale_v, conv_b_v, norm_g_v, norm_b_v = vec(pool_scale), vec(conv_b), vec(conv_norm_g), vec(conv_norm_b)
    final_g = final_norm.reshape(1, d)
    conv_head_dim = d_conv // CONV_HEADS
    conv_wb = jnp.broadcast_to(conv_w.reshape(depth, conv_hist + 1, CONV_HEADS, 1, conv_head_dim),
                               (depth, conv_hist + 1, CONV_HEADS, SUBLANE, conv_head_dim))

    mem = mem_prompt.reshape(batch * n_mem, d)
    state_pool_t = state_pool.transpose(0, 2, 1, 3)
    state_conv_t = state_conv.transpose(0, 2, 1, 3)

    hp = x_prompt.reshape(batch * seq, d)
    hs = x_sample.reshape(bs * steps, d)
    pool_p, conv_p, mk_p, mv_p, pool_s, conv_s = [], [], [], [], [], []

    for l in range(depth):
        last = l == depth - 1
        hs, *ffn_wb = _ffn(hs, ffn_norms[0], *ffn_w[0], l)
        hp = _ffn(hp, ffn_norms[0], *ffn_wb, l)

        p, glu = _proj_in(hs, mix_norm_v, w_in_b, l, d_pool, d_conv)
        po, co, pool_new, conv_new = _mix_sample(state_pool_t, p.reshape(bs, steps, d_pool), state_conv_t,
                                                 glu.reshape(bs, steps, d_conv), pool_w_b, pool_scale_v, conv_w,
                                                 conv_b_v, norm_g_v, norm_b_v, l)
        pool_s.append(pool_new)
        conv_s.append(conv_new)
        hs, w_out_b = _matmul_res(hs, [po.reshape(bs * steps, d_pool), co.reshape(bs * steps, d_conv)], w_out, l,
                                  emit=True)
        hp, pool_tail, conv_tail = _mix_prompt(hp, batch, seq, mix_norm_v, w_in_b, pool_w_b, pool_scale_v, conv_wb,
                                               conv_b_v, norm_g_v, norm_b_v, w_out_b, l, 0, pool_hist, conv_hist)
        pool_p.append(pool_tail)
        conv_p.append(conv_tail)

        q, w_q_b = _rms_proj(hs, xattn_norm_v, w_q, l, emit=True)
        o = _attn_sample(q.reshape(bs, steps, d), cache_mem_k, cache_mem_v, l)
        hs, w_o_b = _matmul_res(hs, [o.reshape(bs * steps, d)], w_o, l, emit=True)
        mk = _rms_proj(mem, mem_norm_v, w_mk, l)
        mv = _rms_proj(mem, mem_norm_v, w_mv, l)
        mk_p.append(mk.reshape(batch, n_mem, heads, head_dim))
        mv_p.append(mv.reshape(batch, n_mem, heads, head_dim))
        hp = _attn_prompt(hp, batch, seq, xattn_norm_v, w_q_b, mk.reshape(batch, n_mem, d),
                          mv.reshape(batch, n_mem, d), w_o_b, l, 0, heads)

        hs, *ffn_wb = _ffn(hs, ffn_norms[1], *ffn_w[1], l, final_g if last else None)
        hp = _ffn(hp, ffn_norms[1], *ffn_wb, l, final_g if last else None)

    return (hp.reshape(batch, seq, d), hs.reshape(bs, steps, d), jnp.stack(pool_p), jnp.stack(conv_p),
            jnp.stack(mk_p), jnp.stack(mv_p), jnp.stack(pool_s).transpose(0, 2, 1, 3),
            jnp.stack(conv_s).transpose(0, 2, 1, 3))
```

```python
import functools

import jax
import jax.numpy as jnp
from jax import lax
from jax.experimental import pallas as pl
from jax.experimental.pallas import tpu as pltpu

F32 = jnp.float32
BF16 = jnp.bfloat16

EPS = 1e-6
PAST_LEN = 16384
POOL_WINDOWS = (2, 4, 8, 16)
CONV_HEADS = 8

LANE = 128
SUBLANE = 8
VMEM_LIMIT = 56 * 1024 * 1024

FFN_TM = 1024
FFN_TF = 1024
FFN_SUB = 256
FFN_EMIT_TF = 256
FFN_VMEM_LIMIT = 60 * 1024 * 1024
MIX_VMEM_LIMIT = 60 * 1024 * 1024
ROW_TM = 512
EMIT_TM = 256
MIX_TS = 512
CONV_CHUNK = 32
SAMPLE_ATTN_BB = 4


def _params(*sem, vmem_limit=VMEM_LIMIT):
    return pltpu.CompilerParams(dimension_semantics=sem, vmem_limit_bytes=vmem_limit)


def _resident(block_shape, index_map):
    return pl.BlockSpec(block_shape, index_map, pipeline_mode=pl.Buffered(1))


def _rms(x, g):
    return x * lax.rsqrt(jnp.mean(x * x, axis=-1, keepdims=True) + EPS) * g


def _silu(x):
    return x * jax.nn.sigmoid(x)


def _tile(n, t):
    t = min(n, t)
    assert n % t == 0, (n, t)
    return t


def _ffn_kernel(x_ref, g_ref, wg_ref, wu_ref, wd_ref, fg_ref, o_ref, *rest, final_norm, d_ff, sub, emit):
    if emit:
        wg_o, wu_o, wd_o, xn_ref = rest
    else:
        (xn_ref,) = rest
    j = pl.program_id(1)
    tf = wg_ref.shape[-1]
    nj = pl.cdiv(d_ff, tf)

    @pl.when(j == 0)
    def _():
        xn_ref[...] = _rms(x_ref[...], g_ref[...]).astype(BF16)
        o_ref[...] = jnp.zeros_like(o_ref)

    def columns(lo, n):
        if emit:
            wg_o[:, 0:n] = wg_ref[0, :, lo:lo + n].astype(BF16)
            wu_o[:, 0:n] = wu_ref[0, :, lo:lo + n].astype(BF16)
            wd_o[0:n, :] = wd_ref[0, lo:lo + n, :].astype(BF16)
            wg, wu, wd, lo = wg_o, wu_o, wd_o, 0
        else:
            wg, wu, wd = wg_ref, wu_ref, wd_ref
        xn = xn_ref[...]
        for c0 in range(lo, lo + n, sub):
            c1 = min(c0 + sub, lo + n)
            gate = jnp.dot(xn, wg[:, c0:c1], preferred_element_type=F32)
            up = jnp.dot(xn, wu[:, c0:c1], preferred_element_type=F32)
            hmid = (_silu(gate) * up).astype(BF16)
            o_ref[...] += jnp.dot(hmid, wd[c0:c1, :], preferred_element_type=F32)

    done = nj * tf - d_ff
    if done == 0:
        columns(0, tf)
    else:
        pl.when(j < nj - 1)(lambda: columns(0, tf))
        pl.when(j == nj - 1)(lambda: columns(done, tf - done))

    @pl.when(j == nj - 1)
    def _():
        y = x_ref[...] + 0.5 * o_ref[...]
        if final_norm:
            y = _rms(y, fg_ref[...])
        o_ref[...] = y


def _ffn(x, norm_g, wg, wu, wd, layer, final_g=None):
    n, d = x.shape
    d_ff = wg.shape[-1]
    emit = wg.ndim == 3
    tm = _tile(n, FFN_TM)
    tf, sub = (FFN_EMIT_TF, FFN_EMIT_TF) if emit else (FFN_TF, FFN_SUB)
    tf = min(tf, d_ff)
    assert tf % LANE == 0 and d_ff % LANE == 0
    final_norm = final_g is not None
    if final_g is None:
        final_g = jnp.ones((1, d), F32)
    start = lambda j: pl.multiple_of(jnp.minimum(j * tf, d_ff - tf), LANE)
    el = pl.Element
    bf16_specs = [
        pl.BlockSpec((el(d), el(tf)), lambda i, j: (0, start(j))),
        pl.BlockSpec((el(d), el(tf)), lambda i, j: (0, start(j))),
        pl.BlockSpec((el(tf), el(d)), lambda i, j: (start(j), 0)),
    ]
    f32_specs = [
        pl.BlockSpec((el(1), el(d), el(tf)), lambda i, j: (layer, 0, start(j))),
        pl.BlockSpec((el(1), el(d), el(tf)), lambda i, j: (layer, 0, start(j))),
        pl.BlockSpec((el(1), el(tf), el(d)), lambda i, j: (layer, start(j), 0)),
    ]
    out_shape = [jax.ShapeDtypeStruct((n, d), F32)]
    out_specs = [pl.BlockSpec((tm, d), lambda i, j: (i, 0))]
    if emit:
        assert n == tm, "the emitted casts are written by a single row tile"
        out_shape += [jax.ShapeDtypeStruct((d, d_ff), BF16), jax.ShapeDtypeStruct((d, d_ff), BF16),
                      jax.ShapeDtypeStruct((d_ff, d), BF16)]
        out_specs += [pl.BlockSpec((d, tf), lambda i, j: (0, j)), pl.BlockSpec((d, tf), lambda i, j: (0, j)),
                      pl.BlockSpec((tf, d), lambda i, j: (j, 0))]
    outs = pl.pallas_call(
        functools.partial(_ffn_kernel, final_norm=final_norm, d_ff=d_ff, sub=sub, emit=emit),
        out_shape=out_shape,
        grid=(n // tm, pl.cdiv(d_ff, tf)),
        in_specs=[
            _resident((tm, d), lambda i, j: (i, 0)),
            pl.BlockSpec((None, 1, d), lambda i, j: (layer, 0, 0)),
            *(f32_specs if emit else bf16_specs),
            pl.BlockSpec((1, d), lambda i, j: (0, 0)),
        ],
        out_specs=out_specs,
        scratch_shapes=[pltpu.VMEM((tm, d), BF16)],
        compiler_params=_params("parallel", "arbitrary", vmem_limit=FFN_VMEM_LIMIT),
        name="ffn_emit" if emit else "ffn",
    )(x, norm_g, wg, wu, wd, final_g)
    return outs if emit else outs[0]


def _proj_in_kernel(x_ref, g_ref, w_ref, p_ref, glu_ref, *, d_pool, d_conv):
    xn = _rms(x_ref[...], g_ref[...]).astype(BF16)
    z = jnp.dot(xn, w_ref[...], preferred_element_type=F32)
    p_ref[...] = z[:, :d_pool]
    glu_ref[...] = z[:, d_pool:d_pool + d_conv] * jax.nn.sigmoid(z[:, d_pool + d_conv:])


def _proj_in(x, norm_g, w_in, layer, d_pool, d_conv):
    n, d = x.shape
    tm = _tile(n, ROW_TM)
    dz = w_in.shape[-1]
    return pl.pallas_call(
        functools.partial(_proj_in_kernel, d_pool=d_pool, d_conv=d_conv),
        out_shape=(jax.ShapeDtypeStruct((n, d_pool), F32), jax.ShapeDtypeStruct((n, d_conv), F32)),
        grid=(n // tm,),
        in_specs=[
            pl.BlockSpec((tm, d), lambda i: (i, 0)),
            pl.BlockSpec((None, 1, d), lambda i: (layer, 0, 0)),
            _resident((None, d, dz), lambda i: (layer, 0, 0)),
        ],
        out_specs=(pl.BlockSpec((tm, d_pool), lambda i: (i, 0)),
                   pl.BlockSpec((tm, d_conv), lambda i: (i, 0))),
        compiler_params=_params("parallel"),
        name="proj_in",
    )(x, norm_g, w_in)


def _bf16_weight(w_ref, wb_ref):
    if not wb_ref:
        return w_ref[...].astype(BF16)
    (wb_ref,) = wb_ref

    @pl.when(pl.program_id(0) == 0)
    def _():
        wb_ref[...] = w_ref[...].astype(BF16)

    return wb_ref[...]


def _emit_weight_out(w):
    return (jax.ShapeDtypeStruct((1,) + w.shape[1:], BF16),
            pl.BlockSpec((None,) + w.shape[1:], lambda i: (0, 0, 0)))


def _rms_proj_kernel(x_ref, g_ref, w_ref, o_ref, *wb_ref):
    xn = _rms(x_ref[...], g_ref[...]).astype(BF16)
    o_ref[...] = jnp.dot(xn, _bf16_weight(w_ref, wb_ref), preferred_element_type=F32)


def _rms_proj(x, norm_g, w, layer, emit=False):
    n, d = x.shape
    dn = w.shape[-1]
    tm = _tile(n, EMIT_TM if emit else ROW_TM)
    out_shape = [jax.ShapeDtypeStruct((n, dn), F32)]
    out_specs = [pl.BlockSpec((tm, dn), lambda i: (i, 0))]
    if emit:
        shape, spec = _emit_weight_out(w)
        out_shape.append(shape)
        out_specs.append(spec)
    outs = pl.pallas_call(
        _rms_proj_kernel,
        out_shape=out_shape,
        grid=(n // tm,),
        in_specs=[
            pl.BlockSpec((tm, d), lambda i: (i, 0)),
            pl.BlockSpec((None, 1, d), lambda i: (layer, 0, 0)),
            _resident((None, d, dn), lambda i: (layer, 0, 0)),
        ],
        out_specs=out_specs,
        compiler_params=_params("arbitrary" if emit else "parallel"),
        name="rms_proj",
    )(x, norm_g, w)
    return outs if emit else outs[0]


def _mem_proj_kernel(x_ref, g_ref, w_ref, o_ref, oh_ref):
    xn = _rms(x_ref[...], g_ref[...]).astype(BF16)
    y = jnp.dot(xn, w_ref[...].astype(BF16), preferred_element_type=F32)
    o_ref[...] = y
    nb, n_mem, heads, hd = oh_ref.shape
    for h in range(heads):
        oh_ref[:, :, h, :] = y[:, h * hd:(h + 1) * hd].reshape(nb, n_mem, hd)


def _mem_proj(mem, norm_g, w, layer, n_mem, heads):
    n, d = mem.shape
    dn = w.shape[-1]
    tm = _tile(n, ROW_TM)
    assert tm % n_mem == 0 and dn % heads == 0
    return pl.pallas_call(
        _mem_proj_kernel,
        out_shape=(jax.ShapeDtypeStruct((n, dn), F32),
                   jax.ShapeDtypeStruct((n // n_mem, n_mem, heads, dn // heads), F32)),
        grid=(n // tm,),
        in_specs=[
            pl.BlockSpec((tm, d), lambda i: (i, 0)),
            pl.BlockSpec((None, 1, d), lambda i: (layer, 0, 0)),
            _resident((None, d, dn), lambda i: (layer, 0, 0)),
        ],
        out_specs=(pl.BlockSpec((tm, dn), lambda i: (i, 0)),
                   pl.BlockSpec((tm // n_mem, n_mem, heads, dn // heads), lambda i: (i, 0, 0, 0))),
        compiler_params=_params("parallel"),
        name="mem_proj",
    )(mem, norm_g, w)


def _matmul_res_kernel(res_ref, *refs, n_x):
    x_refs, w_ref, o_ref, wb_ref = refs[:n_x], refs[n_x], refs[n_x + 1], refs[n_x + 2:]
    w = _bf16_weight(w_ref, wb_ref)
    acc = res_ref[...]
    k0 = 0
    for x_ref in x_refs:
        k1 = k0 + x_ref.shape[-1]
        acc = acc + jnp.dot(x_ref[...].astype(BF16), w[k0:k1, :], preferred_element_type=F32)
        k0 = k1
    o_ref[...] = acc


def _matmul_res(res, xs, w, layer, emit=False):
    n, d = res.shape
    dk = sum(x.shape[-1] for x in xs)
    assert dk == w.shape[1]
    tm = _tile(n, EMIT_TM if emit else ROW_TM)
    out_shape = [jax.ShapeDtypeStruct((n, d), F32)]
    out_specs = [pl.BlockSpec((tm, d), lambda i: (i, 0))]
    if emit:
        shape, spec = _emit_weight_out(w)
        out_shape.append(shape)
        out_specs.append(spec)
    outs = pl.pallas_call(
        functools.partial(_matmul_res_kernel, n_x=len(xs)),
        out_shape=out_shape,
        grid=(n // tm,),
        in_specs=[
            pl.BlockSpec((tm, d), lambda i: (i, 0)),
            *[pl.BlockSpec((tm, x.shape[-1]), lambda i: (i, 0)) for x in xs],
            _resident((None, dk, d), lambda i: (layer, 0, 0)),
        ],
        out_specs=out_specs,
        compiler_params=_params("arbitrary" if emit else "parallel"),
        name="matmul_res",
    )(res, *xs, w)
    return outs if emit else outs[0]


def _head_norm_swish(y, g, b):
    yc = y - jnp.mean(y, axis=-1, keepdims=True)
    var = jnp.mean(yc * yc, axis=-1, keepdims=True)
    return _silu(yc * lax.rsqrt(var + EPS) * g + b)


def _mix_prompt_kernel(h_ref, mg_ref, wi_ref, pw_ref, ps_ref, cw_ref, cb_ref, ng_ref, nb_ref, wo_ref,
                       o_ref, pst_ref, cst_ref, xn_ref, pext, pool_out, *slabs, ts, conv_hist):
    t = pl.program_id(1)
    d_pool = pext.shape[-1]
    pool_group = d_pool // len(POOL_WINDOWS)
    heads = len(slabs) // 2
    gs, ys = slabs[:heads], slabs[heads:]
    hd = gs[0].shape[-1]
    d_conv = heads * hd
    conv_width = conv_hist + 1
    ph = pext.shape[0] - ts
    gh = gs[0].shape[0] - ts

    @pl.when(t == 0)
    def _():
        pext[0:ph, :] = jnp.zeros((ph, d_pool), F32)
        for c in range(heads):
            gs[c][0:gh, :] = jnp.zeros((gh, hd), F32)

    xn_ref[...] = _rms(h_ref[...], mg_ref[...]).astype(BF16)
    pext[ph:ph + ts, :] = jnp.dot(xn_ref[...], wi_ref[:, 0:d_pool], preferred_element_type=F32)

    pos = t * ts + lax.broadcasted_iota(jnp.int32, (ts, 1), 0)
    for g, w in enumerate(POOL_WINDOWS):
        sl = slice(g * pool_group, (g + 1) * pool_group)
        cur = pext[ph:ph + ts, sl]
        acc = cur
        for k in range(1, w):
            acc = acc + pext[ph - k:ph - k + ts, sl]
        cnt = jnp.minimum(pos + 1, w).astype(F32)
        dlt = (acc / cnt - cur).astype(BF16)
        y = jnp.dot(dlt, pw_ref[g], preferred_element_type=F32) * ps_ref[:, sl]
        pool_out[:, sl] = y.astype(BF16)

    groups = CONV_CHUNK // (2 * SUBLANE)

    def conv_head(c):
        hs = slice(c * hd, (c + 1) * hd)
        bias = jnp.broadcast_to(cb_ref[:, hs], (SUBLANE, hd))
        for r0 in range(0, ts, CONV_CHUNK):
            even = [bias] * groups
            odd = [bias] * groups
            for j in range(conv_width + 1):
                rows = [gs[c][pl.ds(r0 + q * 2 * SUBLANE + (gh - conv_hist) + j, SUBLANE, stride=2), :]
                        for q in range(groups)]
                if j < conv_width:
                    wk = cw_ref[j, c]
                    even = [a + x * wk for a, x in zip(even, rows)]
                if j >= 1:
                    wk = cw_ref[j - 1, c]
                    odd = [a + x * wk for a, x in zip(odd, rows)]
            for q in range(groups):
                for phase, acc in enumerate((even[q], odd[q])):
                    y = _head_norm_swish(acc, ng_ref[:, hs], nb_ref[:, hs])
                    ys[c][pl.ds(r0 + q * 2 * SUBLANE + phase, SUBLANE, stride=2), :] = y

    def glu_pair(c0):
        lo = d_pool + c0 * hd
        xn = xn_ref[...]
        a = jnp.dot(xn, wi_ref[:, lo:lo + 2 * hd], preferred_element_type=F32)
        b = jnp.dot(xn, wi_ref[:, lo + d_conv:lo + d_conv + 2 * hd], preferred_element_type=F32)
        glu = a * jax.nn.sigmoid(b)
        gs[c0][gh:gh + ts, :] = glu[:, 0:hd]
        gs[c0 + 1][gh:gh + ts, :] = glu[:, hd:2 * hd]

    o_ref[...] = h_ref[...] + jnp.dot(pool_out[...], wo_ref[0:d_pool, :], preferred_element_type=F32)
    glu_pair(0)
    for c0 in range(0, heads, 2):
        if c0 + 2 < heads:
            glu_pair(c0 + 2)
        conv_head(c0)
        conv_head(c0 + 1)
        pair = jnp.concatenate([ys[c0][...], ys[c0 + 1][...]], axis=1).astype(BF16)
        lo = d_pool + c0 * hd
        o_ref[...] += jnp.dot(pair, wo_ref[lo:lo + 2 * hd, :], preferred_element_type=F32)

    pst_ref[...] = pext[ts:ts + ph, :]
    pext[0:ph, :] = pext[ts:ts + ph, :]
    for c in range(heads):
        cst_ref[:, c * hd:(c + 1) * hd] = gs[c][ts:ts + gh, :]
        gs[c][0:gh, :] = gs[c][ts:ts + gh, :]


def _mix_prompt(h, batch, seq, mix_norm, w_in, pool_w, pool_scale, conv_wb, conv_b, norm_g, norm_b, w_out, layer,
                w_out_layer, pool_hist, conv_hist):
    n, d = h.shape
    d_pool = pool_scale.shape[-1]
    heads, hd = conv_wb.shape[2], conv_wb.shape[4]
    d_conv = heads * hd
    assert hd == LANE and w_in.shape[-1] == d_pool + 2 * d_conv
    ts = _tile(seq, MIX_TS)
    assert ts % CONV_CHUNK == 0
    nt = seq // ts
    ph = -(-pool_hist // SUBLANE) * SUBLANE
    gh = -(-conv_hist // SUBLANE) * SUBLANE
    row = lambda b, t: (b * nt + t, 0)
    vec = lambda b, t: (layer, 0, 0)
    out, pool_tail, conv_tail = pl.pallas_call(
        functools.partial(_mix_prompt_kernel, ts=ts, conv_hist=conv_hist),
        out_shape=(jax.ShapeDtypeStruct((n, d), F32), jax.ShapeDtypeStruct((batch, ph, d_pool), F32),
                   jax.ShapeDtypeStruct((batch, gh, d_conv), F32)),
        grid=(batch, nt),
        in_specs=[
            pl.BlockSpec((ts, d), row),
            pl.BlockSpec((None, 1, d), vec),
            _resident((None, d, d_pool + 2 * d_conv), vec),
            _resident((None,) + pool_w.shape[1:], lambda b, t: (layer, 0, 0, 0)),
            pl.BlockSpec((None, 1, d_pool), vec),
            _resident((None,) + conv_wb.shape[1:], lambda b, t: (layer, 0, 0, 0, 0)),
            pl.BlockSpec((None, 1, d_conv), vec),
            pl.BlockSpec((None, 1, d_conv), vec),
            pl.BlockSpec((None, 1, d_conv), vec),
            _resident((None, d_pool + d_conv, d), lambda b, t: (w_out_layer, 0, 0)),
        ],
        out_specs=(pl.BlockSpec((ts, d), row),
                   pl.BlockSpec((None, ph, d_pool), lambda b, t: (b, 0, 0)),
                   pl.BlockSpec((None, gh, d_conv), lambda b, t: (b, 0, 0))),
        scratch_shapes=[
            pltpu.VMEM((ts, d), BF16),
            pltpu.VMEM((ph + ts, d_pool), F32),
            pltpu.VMEM((ts, d_pool), BF16),
            *[pltpu.VMEM((gh + ts, hd), F32) for _ in range(heads)],
            *[pltpu.VMEM((ts, hd), F32) for _ in range(heads)],
        ],
        compiler_params=_params("arbitrary", "arbitrary", vmem_limit=MIX_VMEM_LIMIT),
        name="mix_prompt",
    )(h, mix_norm, w_in, pool_w, pool_scale, conv_wb, conv_b, norm_g, norm_b, w_out)
    return out, pool_tail[:, ph - pool_hist:], conv_tail[:, gh - conv_hist:]


def _mix_sample_kernel(sp_ref, p_ref, sc_ref, glu_ref, pw_ref, ps_ref, cw_ref, cb_ref, ng_ref, nb_ref,
                       po_ref, co_ref, pn_ref, cn_ref, pe, ge, dbuf, *, steps, pool_hist, conv_hist):
    g = pl.program_id(0)
    _, bs, cb = pe.shape
    head_dim = ng_ref.shape[-1] * len(POOL_WINDOWS) // CONV_HEADS
    conv_width = conv_hist + 1

    pe[0:pool_hist] = sp_ref[...]
    ge[0:conv_hist] = sc_ref[...]
    for t in range(steps):
        pe[pool_hist + t] = p_ref[:, t, :]
        ge[conv_hist + t] = glu_ref[:, t, :]

    for t in range(steps):
        cur = pe[pool_hist + t]
        acc = cur
        dlt = jnp.zeros_like(cur)
        k = 1
        for gi, w in enumerate(POOL_WINDOWS):
            while k < w:
                acc = acc + pe[pool_hist + t - k]
                k += 1
            cnt = float(min(PAST_LEN + t + 1, w))
            dlt = jnp.where(g == gi, acc / cnt - cur, dlt)
        dbuf[t] = dlt.astype(BF16)

        y = jnp.broadcast_to(cb_ref[...], (bs, cb))
        for k in range(conv_width):
            y = y + ge[t + k] * cw_ref[k:k + 1, :]
        for hh in range(cb // head_dim):
            hs = slice(hh * head_dim, (hh + 1) * head_dim)
            co_ref[:, t, hs] = _head_norm_swish(y[:, hs], ng_ref[:, hs], nb_ref[:, hs])

    dall = dbuf[...].reshape(steps * bs, cb)
    y = jnp.dot(dall, pw_ref[...], preferred_element_type=F32) * ps_ref[...]
    for t in range(steps):
        po_ref[:, t, :] = y[t * bs:(t + 1) * bs, :]

    pn_ref[...] = pe[steps:steps + pool_hist]
    cn_ref[...] = ge[steps:steps + conv_hist]


def _mix_sample(state_pool, p, state_conv, glu, pool_w, pool_scale, conv_w, conv_b, norm_g, norm_b, layer):
    _, pool_hist, bs, d_pool = state_pool.shape
    _, conv_hist, _, d_conv = state_conv.shape
    steps = p.shape[1]
    ng = len(POOL_WINDOWS)
    cb = d_pool // ng
    assert d_conv // ng == cb and cb % (d_conv // CONV_HEADS) == 0
    vec = lambda g: (layer, 0, g)
    seq = lambda rows: pl.BlockSpec((bs, rows, cb), lambda g: (0, 0, g))
    slab = lambda rows: pl.BlockSpec((rows, bs, cb), lambda g: (0, 0, g))
    return pl.pallas_call(
        functools.partial(_mix_sample_kernel, steps=steps, pool_hist=pool_hist, conv_hist=conv_hist),
        out_shape=(jax.ShapeDtypeStruct((bs, steps, d_pool), F32), jax.ShapeDtypeStruct((bs, steps, d_conv), F32),
                   jax.ShapeDtypeStruct((pool_hist, bs, d_pool), F32),
                   jax.ShapeDtypeStruct((conv_hist, bs, d_conv), F32)),
        grid=(ng,),
        in_specs=[
            pl.BlockSpec((None, pool_hist, bs, cb), lambda g: (layer, 0, 0, g)),
            seq(steps),
            pl.BlockSpec((None, conv_hist, bs, cb), lambda g: (layer, 0, 0, g)),
            seq(steps),
            pl.BlockSpec((None, None, cb, cb), lambda g: (layer, g, 0, 0)),
            pl.BlockSpec((None, 1, cb), vec),
            pl.BlockSpec((None, conv_hist + 1, cb), vec),
            pl.BlockSpec((None, 1, cb), vec),
            pl.BlockSpec((None, 1, cb), vec),
            pl.BlockSpec((None, 1, cb), vec),
        ],
        out_specs=(seq(steps), seq(steps), slab(pool_hist), slab(conv_hist)),
        scratch_shapes=[pltpu.VMEM((pool_hist + steps, bs, cb), F32), pltpu.VMEM((conv_hist + steps, bs, cb), F32),
                        pltpu.VMEM((steps, bs, cb), BF16)],
        compiler_params=_params("parallel"),
        name="mix_sample",
    )(state_pool, p, state_conv, glu, pool_w, pool_scale, conv_w, conv_b, norm_g, norm_b)


def _softmax(s):
    e = jnp.exp(s - jnp.max(s, axis=-1, keepdims=True))
    return e / jnp.sum(e, axis=-1, keepdims=True)


def _attn_prompt_kernel(h_ref, g_ref, wq_ref, k_ref, v_ref, wo_ref, o_ref, *, heads):
    x = h_ref[...]
    d = x.shape[-1]
    hd = d // heads
    q = jnp.dot(_rms(x, g_ref[...]).astype(BF16), wq_ref[...], preferred_element_type=F32)
    acc = x
    for hh in range(heads):
        hs = slice(hh * hd, (hh + 1) * hd)
        s = lax.dot_general(q[:, hs].astype(BF16), k_ref[:, hs].astype(BF16),
                            (((1,), (1,)), ((), ())), preferred_element_type=F32) * (hd ** -0.5)
        p = _softmax(s).astype(BF16)
        oh = jnp.dot(p, v_ref[:, hs].astype(BF16), preferred_element_type=F32)
        acc = acc + jnp.dot(oh.astype(BF16), wo_ref[hs, :], preferred_element_type=F32)
    o_ref[...] = acc


def _attn_prompt(h, batch, seq, norm_g, w_q, mk, mv, w_o, layer, w_layer, heads):
    n, d = h.shape
    n_mem = mk.shape[1]
    tq = _tile(seq, ROW_TM)
    nt = seq // tq
    row = lambda b, t: (b * nt + t, 0)
    return pl.pallas_call(
        functools.partial(_attn_prompt_kernel, heads=heads),
        out_shape=jax.ShapeDtypeStruct((n, d), F32),
        grid=(batch, nt),
        in_specs=[
            pl.BlockSpec((tq, d), row),
            pl.BlockSpec((None, 1, d), lambda b, t: (layer, 0, 0)),
            _resident((None, d, d), lambda b, t: (w_layer, 0, 0)),
            pl.BlockSpec((None, n_mem, d), lambda b, t: (b, 0, 0)),
            pl.BlockSpec((None, n_mem, d), lambda b, t: (b, 0, 0)),
            _resident((None, d, d), lambda b, t: (w_layer, 0, 0)),
        ],
        out_specs=pl.BlockSpec((tq, d), row),
        compiler_params=_params("parallel", "parallel"),
        name="attn_prompt",
    )(h, norm_g, w_q, mk, mv, w_o)


def _attn_sample_kernel(q_ref, k_ref, v_ref, o_ref):
    bb, t, _ = q_ref.shape
    _, n_mem, heads, hd = k_ref.shape
    q = q_ref[...]
    qh = jnp.concatenate([q[:, :, h * hd:(h + 1) * hd] for h in range(heads)], axis=1).astype(BF16)
    k = k_ref[...].reshape(bb, n_mem * heads, hd).astype(BF16)
    v = v_ref[...].reshape(bb, n_mem * heads, hd).astype(BF16)
    s = jnp.einsum("bqd,bkd->bqk", qh, k, preferred_element_type=F32) * (hd ** -0.5)
    q_head = lax.broadcasted_iota(jnp.int32, s.shape, 1) // t
    k_head = lax.broadcasted_iota(jnp.int32, s.shape, 2) % heads
    p = _softmax(jnp.where(q_head == k_head, s, -jnp.inf)).astype(BF16)
    o = jnp.einsum("bqk,bkd->bqd", p, v, preferred_element_type=F32)
    for h in range(heads):
        o_ref[:, :, h * hd:(h + 1) * hd] = o[:, h * t:(h + 1) * t, :]


def _attn_sample(q, cache_k, cache_v, layer):
    bs, t, d = q.shape
    _, _, n_mem, heads, hd = cache_k.shape
    bb = _tile(bs, SAMPLE_ATTN_BB)
    kv = pl.BlockSpec((None, bb, n_mem, heads, hd), lambda i: (layer, i, 0, 0, 0))
    return pl.pallas_call(
        _attn_sample_kernel,
        out_shape=jax.ShapeDtypeStruct((bs, t, d), F32),
        grid=(bs // bb,),
        in_specs=[pl.BlockSpec((bb, t, d), lambda i: (i, 0, 0)), kv, kv],
        out_specs=pl.BlockSpec((bb, t, d), lambda i: (i, 0, 0)),
        compiler_params=_params("parallel"),
        name="attn_sample",
    )(q, cache_k, cache_v)


def kernel(x_prompt, x_sample, state_pool, state_conv, cache_mem_k, cache_mem_v, mem_prompt, ffn1_norm, ffn1_w_gate, ffn1_w_up, ffn1_w_down, mix_norm, w_in, pool_w, pool_scale, conv_w, conv_b, conv_norm_g, conv_norm_b, w_out, xattn_norm, mem_norm, w_q, w_mk, w_mv, w_o, ffn2_norm, ffn2_w_gate, ffn2_w_up, ffn2_w_down, final_norm):
    batch, seq, d = x_prompt.shape
    bs, steps, _ = x_sample.shape
    depth = state_pool.shape[0]
    pool_hist, d_pool = state_pool.shape[2:]
    conv_hist, d_conv = state_conv.shape[2:]
    n_mem, heads, head_dim = cache_mem_k.shape[2:]

    bf = lambda w: w.astype(BF16)
    ffn_w = ((ffn1_w_gate, ffn1_w_up, ffn1_w_down), (ffn2_w_gate, ffn2_w_up, ffn2_w_down))
    w_in_b, pool_w_b = bf(w_in), bf(pool_w)

    vec = lambda v: v.reshape(v.shape[0], 1, v.shape[1])
    ffn_norms = (vec(ffn1_norm), vec(ffn2_norm))
    mix_norm_v, xattn_norm_v, mem_norm_v = vec(mix_norm), vec(xattn_norm), vec(mem_norm)
    pool_scale_v, conv_b_v, norm_g_v, norm_b_v = vec(pool_scale), vec(conv_b), vec(conv_norm_g), vec(conv_norm_b)
    final_g = final_norm.reshape(1, d)
    conv_head_dim = d_conv // CONV_HEADS
    conv_wb = jnp.broadcast_to(conv_w.reshape(depth, conv_hist + 1, CONV_HEADS, 1, conv_head_dim),
                               (depth, conv_hist + 1, CONV_HEADS, SUBLANE, conv_head_dim))

    mem = mem_prompt.reshape(batch * n_mem, d)
    state_pool_t = state_pool.transpose(0, 2, 1, 3)
    state_conv_t = state_conv.transpose(0, 2, 1, 3)

    hp = x_prompt.reshape(batch * seq, d)
    hs = x_sample.reshape(bs * steps, d)
    pool_p, conv_p, mk_p, mv_p, pool_s, conv_s = [], [], [], [], [], []

    for l in range(depth):
        last = l == depth - 1
        hs, *ffn_wb = _ffn(hs, ffn_norms[0], *ffn_w[0], l)
        hp = _ffn(hp, ffn_norms[0], *ffn_wb, l)

        p, glu = _proj_in(hs, mix_norm_v, w_in_b, l, d_pool, d_conv)
        po, co, pool_new, conv_new = _mix_sample(state_pool_t, p.reshape(bs, steps, d_pool), state_conv_t,
                                                 glu.reshape(bs, steps, d_conv), pool_w_b, pool_scale_v, conv_w,
                                                 conv_b_v, norm_g_v, norm_b_v, l)
        pool_s.append(pool_new)
        conv_s.append(conv_new)
        hs, w_out_b = _matmul_res(hs, [po.reshape(bs * steps, d_pool), co.reshape(bs * steps, d_conv)], w_out, l,
                                  emit=True)
        hp, pool_tail, conv_tail = _mix_prompt(hp, batch, seq, mix_norm_v, w_in_b, pool_w_b, pool_scale_v, conv_wb,
                                               conv_b_v, norm_g_v, norm_b_v, w_out_b, l, 0, pool_hist, conv_hist)
        pool_p.append(pool_tail)
        conv_p.append(conv_tail)

        q, w_q_b = _rms_proj(hs, xattn_norm_v, w_q, l, emit=True)
        o = _attn_sample(q.reshape(bs, steps, d), cache_mem_k, cache_mem_v, l)
        hs, w_o_b = _matmul_res(hs, [o.reshape(bs * steps, d)], w_o, l, emit=True)
        mk, mk_heads = _mem_proj(mem, mem_norm_v, w_mk, l, n_mem, heads)
        mv, mv_heads = _mem_proj(mem, mem_norm_v, w_mv, l, n_mem, heads)
        mk_p.append(mk_heads)
        mv_p.append(mv_heads)
        hp = _attn_prompt(hp, batch, seq, xattn_norm_v, w_q_b, mk.reshape(batch, n_mem, d),
                          mv.reshape(batch, n_mem, d), w_o_b, l, 0, heads)

        hs, *ffn_wb = _ffn(hs, ffn_norms[1], *ffn_w[1], l, final_g if last else None)
        hp = _ffn(hp, ffn_norms[1], *ffn_wb, l, final_g if last else None)

    return (hp.reshape(batch, seq, d), hs.reshape(bs, steps, d), jnp.stack(pool_p), jnp.stack(conv_p),
            jnp.stack(mk_p), jnp.stack(mv_p), jnp.stack(pool_s).transpose(0, 2, 1, 3),
            jnp.stack(conv_s).transpose(0, 2, 1, 3))
```
